```python
import math
import jax, jax.numpy as jnp
from jax import lax
import numpy as np

D_MODEL = 1024
BATCH = 32
SEQ = 2048
DEPTH = 4
DEC_BATCH = 8
DEC_SEQ = 32
PAST_LEN = 1024

CHUNK = 64
N_A = DEPTH // 2
N_B = DEPTH - N_A
POOL_WINDOWS = (2, 4, 8, 16)
N_POOL_GROUPS = len(POOL_WINDOWS)
POOL_GROUP = D_MODEL // N_POOL_GROUPS
POOL_STATE = max(POOL_WINDOWS) - 1
N_HEADS = D_MODEL // 128
QK_NOPE = 128
QK_ROPE = 64
V_HEAD = 128
KV_LORA = D_MODEL // 4
Q_LORA = 3 * D_MODEL // 8
D_FF = ((8 * D_MODEL // 3 + 127) // 128) * 128
ROPE_BASE = 10000.0
Q_BLOCK = 128
EPS = 1e-6
ATTN_SCALE = (QK_NOPE + QK_ROPE) ** -0.5

kernel_name = "yoco_pool_mla_streaming_step"


def rms_norm(x, g):
    xf = x.astype(jnp.float32)
    y = xf * lax.rsqrt(jnp.mean(xf * xf, axis=-1, keepdims=True) + EPS)
    return (y * g.astype(jnp.float32)).astype(x.dtype)


def swiglu(h, w_in, w_out):
    gate, up = jnp.split(h @ w_in, 2, axis=-1)
    return (jax.nn.silu(gate) * up) @ w_out


def rope(x, pos):
    half = x.shape[-1] // 2
    inv = ROPE_BASE ** (-(jnp.arange(0, x.shape[-1], 2, dtype=jnp.float32) / x.shape[-1]))
    ang = pos.astype(jnp.float32)[:, None] * inv[None, :]
    cos = jnp.cos(ang)[None, :, None, :]
    sin = jnp.sin(ang)[None, :, None, :]
    xf = x.astype(jnp.float32)
    x1, x2 = xf[..., :half], xf[..., half:]
    return jnp.concatenate([x1 * cos - x2 * sin, x2 * cos + x1 * sin], axis=-1).astype(x.dtype)


def pool_mix(h, prev, start, w_pool, scale):
    B, S, _ = h.shape
    ext = jnp.concatenate([prev.astype(h.dtype), h], axis=1)
    pos_ext = start - POOL_STATE + jnp.arange(POOL_STATE + S)
    extf = jnp.where((pos_ext >= 0)[None, :, None], ext.astype(jnp.float32), 0.0)
    cs = jnp.concatenate([jnp.zeros((B, 1, D_MODEL), jnp.float32), jnp.cumsum(extf, axis=1)], axis=1)
    pos = start + jnp.arange(S)
    end = cs[:, POOL_STATE + 1:]
    hf = h.astype(jnp.float32)
    diffs = []
    for g, w in enumerate(POOL_WINDOWS):
        lo, hi = g * POOL_GROUP, (g + 1) * POOL_GROUP
        begin = cs[:, POOL_STATE + 1 - w:POOL_STATE + 1 - w + S, lo:hi]
        cnt = jnp.minimum(pos + 1, w).astype(jnp.float32)[None, :, None]
        diffs.append((end[..., lo:hi] - begin) / cnt - hf[..., lo:hi])
    d = jnp.stack(diffs, axis=2).astype(h.dtype)
    out = jnp.einsum('bsgc,gcd->bsgd', d, w_pool).reshape(B, S, D_MODEL)
    return out * scale, ext[:, -POOL_STATE:]


def mla_shared_latent(x, pos, kv_norm, w_dkv, c_norm, w_kr, kr_norm):
    h = rms_norm(x, kv_norm)
    c = rms_norm(h @ w_dkv, c_norm)
    kr = rope(rms_norm(h @ w_kr, kr_norm)[:, :, None, :], pos)[:, :, 0]
    return c, kr


def mla_shared_kv(c_all, w_uk, kn_norm, w_uv):
    B, T, _ = c_all.shape
    k_nope = rms_norm((c_all @ w_uk).reshape(B, T, N_HEADS, QK_NOPE), kn_norm)
    v = (c_all @ w_uv).reshape(B, T, N_HEADS, V_HEAD)
    return k_nope, v


def mla_attend(h, pos, start, P, k_nope, k_rope, v, w_dq, q_lat_norm, w_uq, qn_norm, qr_norm, w_o):
    B, S, _ = h.shape
    q_lat = rms_norm(h @ w_dq, q_lat_norm)
    q = (q_lat @ w_uq).reshape(B, S, N_HEADS, QK_NOPE + QK_ROPE)
    q_nope = rms_norm(q[..., :QK_NOPE], qn_norm)
    q_rope = rope(rms_norm(q[..., QK_NOPE:], qr_norm), pos)
    outs = []
    for qb in range(-(-S // Q_BLOCK)):
        q0 = qb * Q_BLOCK
        q1 = min(S, q0 + Q_BLOCK)
        nk = P + q1
        s = (jnp.einsum('bqhd,bkhd->bhqk', q_nope[:, q0:q1], k_nope[:, :nk], preferred_element_type=jnp.float32)
             + jnp.einsum('bqhr,bkr->bhqk', q_rope[:, q0:q1], k_rope[:, :nk], preferred_element_type=jnp.float32)) * ATTN_SCALE
        qc = (start + jnp.arange(q0, q1)) // CHUNK
        kc = (start - P + jnp.arange(nk)) // CHUNK
        s = jnp.where((kc[None, :] <= qc[:, None])[None, None], s, -jnp.inf)
        p = jax.nn.softmax(s, axis=-1)
        outs.append(jnp.einsum('bhqk,bkhd->bqhd', p.astype(v.dtype), v[:, :nk]))
    o = jnp.concatenate(outs, axis=1).reshape(B, S, N_HEADS * V_HEAD)
    return o @ w_o


def trunk(x, pool_prev, ckv_past, kr_past, start,
          ffn1_norm, ffn1_w_in, ffn1_w_out, mix_norm, ffn2_norm, ffn2_w_in, ffn2_w_out,
          pool_w, pool_scale, kv_norm, w_dkv, c_norm, w_kr, kr_norm, w_uk, kn_norm, w_uv,
          w_dq, q_lat_norm, w_uq, qn_norm, qr_norm, w_o):
    B, S, _ = x.shape
    P = ckv_past.shape[1]
    pos = start + jnp.arange(S, dtype=jnp.int32)
    new_pool = []
    c_new = kr_new = k_nope = k_rope = v = None
    for layer in range(DEPTH):
        if layer == N_A:
            c_new, kr_new = mla_shared_latent(x, pos, kv_norm, w_dkv, c_norm, w_kr, kr_norm)
            c_all = jnp.concatenate([ckv_past.astype(c_new.dtype), c_new], axis=1)
            k_rope = jnp.concatenate([kr_past.astype(kr_new.dtype), kr_new], axis=1)
            k_nope, v = mla_shared_kv(c_all, w_uk, kn_norm, w_uv)
        x = x + 0.5 * swiglu(rms_norm(x, ffn1_norm[layer]), ffn1_w_in[layer], ffn1_w_out[layer])
        h = rms_norm(x, mix_norm[layer])
        if layer < N_A:
            m, st = pool_mix(h, pool_prev[layer], start, pool_w[layer], pool_scale[layer])
            new_pool.append(st)
        else:
            i = layer - N_A
            m = mla_attend(h, pos, start, P, k_nope, k_rope, v, w_dq[i], q_lat_norm[i], w_uq[i],
                           qn_norm[i], qr_norm[i], w_o[i])
        x = x + m
        x = x + 0.5 * swiglu(rms_norm(x, ffn2_norm[layer]), ffn2_w_in[layer], ffn2_w_out[layer])
    return x, jnp.stack(new_pool, axis=0), c_new, kr_new


def setup_inputs(seed: int = 0) -> dict:
    key = jax.random.key(seed)
    ks = jax.random.split(key, 40)
    f32 = jnp.float32
    nrm = lambda k, shape, s: jax.random.normal(k, shape, f32) * s
    gain = lambda k, shape: 1.0 + 0.01 * jax.random.normal(k, shape, f32)
    HQ = N_HEADS * (QK_NOPE + QK_ROPE)
    return {
        'x_prompt': nrm(ks[0], (BATCH, SEQ, D_MODEL), 1.0),
        'x_sample': nrm(ks[1], (DEC_BATCH, DEC_SEQ, D_MODEL), 1.0),
        'state_pool': nrm(ks[2], (N_A, DEC_BATCH, POOL_STATE, D_MODEL), 1.0),
        'cache_ckv': nrm(ks[3], (DEC_BATCH, PAST_LEN, KV_LORA), 1.0),
        'cache_krope': nrm(ks[4], (DEC_BATCH, PAST_LEN, QK_ROPE), 1.0),
        'ffn1_norm': gain(ks[5], (DEPTH, D_MODEL)),
        'ffn1_w_in': nrm(ks[6], (DEPTH, D_MODEL, 2 * D_FF), D_MODEL ** -0.5),
        'ffn1_w_out': nrm(ks[7], (DEPTH, D_FF, D_MODEL), D_FF ** -0.5),
        'mix_norm': gain(ks[8], (DEPTH, D_MODEL)),
        'ffn2_norm': gain(ks[9], (DEPTH, D_MODEL)),
        'ffn2_w_in': nrm(ks[10], (DEPTH, D_MODEL, 2 * D_FF), D_MODEL ** -0.5),
        'ffn2_w_out': nrm(ks[11], (DEPTH, D_FF, D_MODEL), D_FF ** -0.5),
        'pool_w': nrm(ks[12], (N_A, N_POOL_GROUPS, POOL_GROUP, POOL_GROUP), POOL_GROUP ** -0.5),
        'pool_scale': 0.5 + 0.05 * jax.random.normal(ks[13], (N_A, D_MODEL), f32),
        'kv_norm': gain(ks[14], (D_MODEL,)),
        'w_dkv': nrm(ks[15], (D_MODEL, KV_LORA), D_MODEL ** -0.5),
        'c_norm': gain(ks[16], (KV_LORA,)),
        'w_kr': nrm(ks[17], (D_MODEL, QK_ROPE), D_MODEL ** -0.5),
        'kr_norm': gain(ks[18], (QK_ROPE,)),
        'w_uk': nrm(ks[19], (KV_LORA, N_HEADS * QK_NOPE), KV_LORA ** -0.5),
        'kn_norm': gain(ks[20], (QK_NOPE,)),
        'w_uv': nrm(ks[21], (KV_LORA, N_HEADS * V_HEAD), KV_LORA ** -0.5),
        'w_dq': nrm(ks[22], (N_B, D_MODEL, Q_LORA), D_MODEL ** -0.5),
        'q_lat_norm': gain(ks[23], (N_B, Q_LORA)),
        'w_uq': nrm(ks[24], (N_B, Q_LORA, HQ), Q_LORA ** -0.5),
        'qn_norm': gain(ks[25], (N_B, QK_NOPE)),
        'qr_norm': gain(ks[26], (N_B, QK_ROPE)),
        'w_o': nrm(ks[27], (N_B, N_HEADS * V_HEAD, D_MODEL), (N_HEADS * V_HEAD) ** -0.5),
    }


def reference(x_prompt, x_sample, state_pool, cache_ckv, cache_krope,
              ffn1_norm, ffn1_w_in, ffn1_w_out, mix_norm, ffn2_norm, ffn2_w_in, ffn2_w_out,
              pool_w, pool_scale, kv_norm, w_dkv, c_norm, w_kr, kr_norm, w_uk, kn_norm, w_uv,
              w_dq, q_lat_norm, w_uq, qn_norm, qr_norm, w_o):
    Bp = x_prompt.shape[0]
    pool_prev_p = jnp.zeros((N_A, Bp, POOL_STATE, D_MODEL), x_prompt.dtype)
    ckv_prev_p = jnp.zeros((Bp, 0, KV_LORA), x_prompt.dtype)
    kr_prev_p = jnp.zeros((Bp, 0, QK_ROPE), x_prompt.dtype)
    y_prompt, new_pool_prompt, new_ckv_prompt, new_krope_prompt = trunk(
        x_prompt, pool_prev_p, ckv_prev_p, kr_prev_p, 0,
        ffn1_norm, ffn1_w_in, ffn1_w_out, mix_norm, ffn2_norm, ffn2_w_in, ffn2_w_out,
        pool_w, pool_scale, kv_norm, w_dkv, c_norm, w_kr, kr_norm, w_uk, kn_norm, w_uv,
        w_dq, q_lat_norm, w_uq, qn_norm, qr_norm, w_o)
    y_sample, new_pool_sample, new_ckv_sample, new_krope_sample = trunk(
        x_sample, state_pool, cache_ckv, cache_krope, cache_ckv.shape[1],
        ffn1_norm, ffn1_w_in, ffn1_w_out, mix_norm, ffn2_norm, ffn2_w_in, ffn2_w_out,
        pool_w, pool_scale, kv_norm, w_dkv, c_norm, w_kr, kr_norm, w_uk, kn_norm, w_uv,
        w_dq, q_lat_norm, w_uq, qn_norm, qr_norm, w_o)
    return (y_prompt, y_sample, new_pool_prompt, new_pool_sample,
            new_ckv_prompt, new_krope_prompt, new_ckv_sample, new_krope_sample)
```

```python
import functools

import numpy as np
import jax
import jax.numpy as jnp
from jax import lax
from jax.experimental import pallas as pl
from jax.experimental.pallas import tpu as pltpu

F32 = jnp.float32
BF16 = jnp.bfloat16

EPS = 1e-6
CHUNK = 64
CHUNK_SHIFT = 6
ROPE_BASE = 10000.0
POOL_WINDOWS = (2, 4, 8, 16)
POOL_HIST = 16
QK_NOPE = 128
QK_ROPE = 64
V_HEAD = 128
LANES = 128
VMEM_LIMIT_BYTES = 56 * 1024 * 1024


def _rms(x, g, n=None):
    n = x.shape[-1] if n is None else n
    ms = jnp.sum(x * x, axis=-1, keepdims=True) * (1.0 / n)
    return x * lax.rsqrt(ms + EPS) * g


def _resident(shape):
    zeros = (0,) * len(shape)
    return pl.BlockSpec(shape, lambda *_: zeros, pipeline_mode=pl.Buffered(1))


def _params(*sem):
    return pltpu.CompilerParams(dimension_semantics=sem, vmem_limit_bytes=VMEM_LIMIT_BYTES)


def _row_tile(t, cap):
    tm = cap
    while t % tm:
        tm //= 2
    assert tm >= 8, (t, cap)
    return tm


def _ffn_body(x, g_ref, win_ref, wout_ref, d_ff):
    h = _rms(x, g_ref[...]).astype(BF16)
    gu = jnp.dot(h, win_ref[...], preferred_element_type=F32)
    a = (jax.nn.silu(gu[:, :d_ff]) * gu[:, d_ff:]).astype(BF16)
    return x + 0.5 * jnp.dot(a, wout_ref[...], preferred_element_type=F32)


def _ffn_kernel(x_ref, g_ref, win_ref, wout_ref, o_ref, *, d_ff):
    o_ref[...] = _ffn_body(x_ref[...], g_ref, win_ref, wout_ref, d_ff)


def _ffn_wo_kernel(x_ref, a_ref, wo_ref, g_ref, win_ref, wout_ref, o_ref, *, d_ff):
    x = x_ref[...] + jnp.dot(a_ref[...], wo_ref[...], preferred_element_type=F32)
    o_ref[...] = _ffn_body(x, g_ref, win_ref, wout_ref, d_ff)


def _ffn(x, g, w_in, w_out, attn=None, w_o=None):
    t, d = x.shape
    d_ff = w_out.shape[0]
    tm = _row_tile(t, 512)
    row = pl.BlockSpec((tm, d), lambda i: (i, 0))
    w_specs = [_resident((1, d)), _resident(w_in.shape), _resident(w_out.shape)]
    if attn is None:
        kern, ins, specs = _ffn_kernel, (x, g, w_in, w_out), [row] + w_specs
    else:
        kern = _ffn_wo_kernel
        ins = (x, attn, w_o, g, w_in, w_out)
        specs = [row, pl.BlockSpec((tm, attn.shape[1]), lambda i: (i, 0)), _resident(w_o.shape)] + w_specs
    return pl.pallas_call(
        functools.partial(kern, d_ff=d_ff),
        out_shape=jax.ShapeDtypeStruct((t, d), F32),
        grid=(t // tm,),
        in_specs=specs,
        out_specs=row,
        compiler_params=_params("parallel"),
        name="ffn" if attn is None else "ffn_wo",
    )(*ins)


def _pool_kernel(x_ref, prev_ref, g_ref, w_ref, sc_ref, o_ref, st_ref, carry_ref, *, start, ts):
    j = pl.program_id(1)

    @pl.when(j == 0)
    def _():
        prev = prev_ref[...]
        if start < POOL_HIST:
            rowpos = start - POOL_HIST + lax.broadcasted_iota(jnp.int32, (POOL_HIST, 1), 0)
            prev = jnp.where(rowpos >= 0, prev, 0.0)
        carry_ref[...] = prev

    x = x_ref[...]
    h = _rms(x, g_ref[...])
    ext = jnp.concatenate([carry_ref[...], h], axis=0)
    carry_ref[...] = h[ts - POOL_HIST:, :]
    st_ref[...] = h[ts - POOL_HIST:, :]
    pos = start + j * ts + lax.broadcasted_iota(jnp.int32, (ts, 1), 0)
    group = w_ref.shape[1]
    outs = []
    for gi, w in enumerate(POOL_WINDOWS):
        lo, hi = gi * group, (gi + 1) * group
        p = ext[:, lo:hi]
        k = 1
        while k < w:
            p = p + pltpu.roll(p, k, axis=0)
            k *= 2
        cnt = jnp.minimum(pos + 1, w).astype(F32)
        dlt = p[POOL_HIST:, :] / cnt - h[:, lo:hi]
        outs.append(jnp.dot(dlt.astype(BF16), w_ref[gi], preferred_element_type=F32))
    o_ref[...] = x + jnp.concatenate(outs, axis=-1) * sc_ref[...]


def _pool_mix(x, prev16, g, w_pool, scale, start):
    b, s, d = x.shape
    ts = _row_tile(s, 512)
    assert ts >= POOL_HIST and len(POOL_WINDOWS) * w_pool.shape[1] == d
    seq = pl.BlockSpec((None, ts, d), lambda bi, j: (bi, j, 0))
    hist = pl.BlockSpec((None, POOL_HIST, d), lambda bi, j: (bi, 0, 0))
    return pl.pallas_call(
        functools.partial(_pool_kernel, start=start, ts=ts),
        out_shape=(jax.ShapeDtypeStruct((b, s, d), F32), jax.ShapeDtypeStruct((b, POOL_HIST, d), F32)),
        grid=(b, s // ts),
        in_specs=[seq, hist, _resident((1, d)), _resident(w_pool.shape), _resident((1, d))],
        out_specs=(seq, hist),
        scratch_shapes=[pltpu.VMEM((POOL_HIST, d), F32)],
        compiler_params=_params("parallel", "arbitrary"),
        name="pool_mix",
    )(x, prev16, g, w_pool, scale)


def _rope_rows(z, z_sw, g, g_sw, cos, sin):
    r = lax.rsqrt(jnp.sum(z * z, axis=-1, keepdims=True) * (1.0 / QK_ROPE) + EPS)
    return (z * r) * g * cos + (z_sw * r) * g_sw * sin


def _latent_kernel(x_ref, g_ref, w_ref, cg_ref, rg_ref, rgs_ref, cos_ref, sin_ref, c_ref, kr_ref, *, kv_lora):
    h = _rms(x_ref[...], g_ref[...]).astype(BF16)
    z = jnp.dot(h, w_ref[...], preferred_element_type=F32)
    c_ref[...] = _rms(z[:, :kv_lora], cg_ref[...])
    kr_ref[...] = _rope_rows(z[:, kv_lora:kv_lora + LANES], z[:, kv_lora + LANES:],
                             rg_ref[...], rgs_ref[...], cos_ref[...], sin_ref[...])


def _table_spec(tm, s):
    if tm <= s:
        assert s % tm == 0
        per = s // tm
        return pl.BlockSpec((tm, LANES), lambda i: (i % per, 0))
    return pl.BlockSpec((tm, LANES), lambda i: (0, 0))


def _tables_for(cos, sin, tm, s):
    if tm <= s:
        return cos, sin
    assert tm % s == 0
    return jnp.tile(cos, (tm // s, 1)), jnp.tile(sin, (tm // s, 1))


def _latent(x, s, g, w_lat, c_g, r_g, r_gs, cos, sin):
    t, d = x.shape
    kv_lora = c_g.shape[1]
    tm = _row_tile(t, 512)
    cos_t, sin_t = _tables_for(cos, sin, tm, s)
    return pl.pallas_call(
        functools.partial(_latent_kernel, kv_lora=kv_lora),
        out_shape=(jax.ShapeDtypeStruct((t, kv_lora), F32), jax.ShapeDtypeStruct((t, LANES), F32)),
        grid=(t // tm,),
        in_specs=[pl.BlockSpec((tm, d), lambda i: (i, 0)), _resident((1, d)), _resident(w_lat.shape),
                  _resident((1, kv_lora)), _resident((1, LANES)), _resident((1, LANES)),
                  _table_spec(tm, s), _table_spec(tm, s)],
        out_specs=(pl.BlockSpec((tm, kv_lora), lambda i: (i, 0)), pl.BlockSpec((tm, LANES), lambda i: (i, 0))),
        compiler_params=_params("parallel"),
        name="mla_latent",
    )(x, g, w_lat, c_g, r_g, r_gs, cos_t, sin_t)


def _kv_kernel(c_ref, kr_ref, w_ref, g_ref, k_ref, v_ref, *, n_heads):
    z = jnp.dot(c_ref[...].astype(BF16), w_ref[...], preferred_element_type=F32)
    kr = kr_ref[...].astype(BF16)
    for hh in range(n_heads):
        kn = _rms(z[:, hh * QK_NOPE:(hh + 1) * QK_NOPE], g_ref[...]).astype(BF16)
        k_ref[hh] = jnp.concatenate([kn, kr], axis=-1)
        off = n_heads * QK_NOPE + hh * V_HEAD
        v_ref[hh] = z[:, off:off + V_HEAD].astype(BF16)


def _kv_expand(c_all, kr_all, w_kv, kn_g, n_heads):
    b, tk, kv_lora = c_all.shape
    tr = tk if tk % 512 else 512
    return pl.pallas_call(
        functools.partial(_kv_kernel, n_heads=n_heads),
        out_shape=(jax.ShapeDtypeStruct((b, n_heads, tk, QK_NOPE + LANES), BF16),
                   jax.ShapeDtypeStruct((b, n_heads, tk, V_HEAD), BF16)),
        grid=(b, tk // tr),
        in_specs=[pl.BlockSpec((None, tr, kv_lora), lambda bi, j: (bi, j, 0)),
                  pl.BlockSpec((None, tr, LANES), lambda bi, j: (bi, j, 0)),
                  _resident(w_kv.shape), _resident((1, QK_NOPE))],
        out_specs=(pl.BlockSpec((None, n_heads, tr, QK_NOPE + LANES), lambda bi, j: (bi, 0, j, 0)),
                   pl.BlockSpec((None, n_heads, tr, V_HEAD), lambda bi, j: (bi, 0, j, 0))),
        compiler_params=_params("parallel", "parallel"),
        name="mla_kv",
    )(c_all, kr_all, w_kv, kn_g)


def _q_kernel(x_ref, g_ref, wdq_ref, lg_ref, wq_ref, ng_ref, rg_ref, rgs_ref, cos_ref, sin_ref, q_ref,
              *, n_heads, scale):
    h = _rms(x_ref[...], g_ref[...]).astype(BF16)
    ql = _rms(jnp.dot(h, wdq_ref[...], preferred_element_type=F32), lg_ref[...]).astype(BF16)
    z = jnp.dot(ql, wq_ref[...], preferred_element_type=F32)
    cos, sin = cos_ref[...], sin_ref[...]
    for hh in range(n_heads):
        qn = _rms(z[:, hh * QK_NOPE:(hh + 1) * QK_NOPE], ng_ref[...])
        o1 = n_heads * QK_NOPE + hh * LANES
        o2 = o1 + n_heads * LANES
        qr = _rope_rows(z[:, o1:o1 + LANES], z[:, o2:o2 + LANES], rg_ref[...], rgs_ref[...], cos, sin)
        q_ref[hh] = jnp.concatenate([qn * scale, qr * scale], axis=-1).astype(BF16)


def _q_proj(x, g, w_dq, l_g, w_q, n_g, r_g, r_gs, cos, sin, n_heads, scale):
    b, s, d = x.shape
    tm = _row_tile(s, 512)
    q_lora = w_dq.shape[1]
    tab = pl.BlockSpec((tm, LANES), lambda bi, j: (j, 0))
    return pl.pallas_call(
        functools.partial(_q_kernel, n_heads=n_heads, scale=scale),
        out_shape=jax.ShapeDtypeStruct((b, n_heads, s, QK_NOPE + LANES), BF16),
        grid=(b, s // tm),
        in_specs=[pl.BlockSpec((None, tm, d), lambda bi, j: (bi, j, 0)), _resident((1, d)),
                  _resident(w_dq.shape), _resident((1, q_lora)), _resident(w_q.shape),
                  _resident((1, QK_NOPE)), _resident((1, LANES)), _resident((1, LANES)), tab, tab],
        out_specs=pl.BlockSpec((None, n_heads, tm, QK_NOPE + LANES), lambda bi, j: (bi, 0, j, 0)),
        compiler_params=_params("parallel", "parallel"),
        name="mla_q",
    )(x, g, w_dq, l_g, w_q, n_g, r_g, r_gs, cos, sin)


def _attn_kernel(q_ref, k_ref, v_ref, o_ref, *, tq, tk, start, past):
    i = pl.program_id(2)
    q = q_ref[...]

    def block(j, carry, masked):
        m, l, acc = carry
        off = pl.multiple_of(j * tk, tk)
        s = lax.dot_general(q, k_ref[pl.ds(off, tk), :], (((1,), (1,)), ((), ())), preferred_element_type=F32)
        if masked:
            qpos = start + i * tq + lax.broadcasted_iota(jnp.int32, (tq, 1), 0)
            kpos = start - past + j * tk + lax.broadcasted_iota(jnp.int32, (1, tk), 1)
            visible = jnp.right_shift(kpos, CHUNK_SHIFT) <= jnp.right_shift(qpos, CHUNK_SHIFT)
            s = jnp.where(visible, s, -jnp.inf)
        m_new = jnp.maximum(m, jnp.max(s, axis=-1, keepdims=True))
        alpha = jnp.exp(m - m_new)
        p = jnp.exp(s - m_new)
        l = alpha * l + jnp.sum(p, axis=-1, keepdims=True)
        pv = jnp.dot(p.astype(BF16), v_ref[pl.ds(off, tk), :], preferred_element_type=F32)
        return m_new, l, alpha * acc + pv

    init = (jnp.full((tq, 1), -jnp.inf, F32), jnp.zeros((tq, 1), F32), jnp.zeros((tq, V_HEAD), F32))
    n_full = _full_blocks(i, tq, tk, start, past)
    carry = lax.fori_loop(0, n_full, lambda j, c: block(j, c, False), init)
    _, l, acc = block(n_full, carry, True)
    o_ref[...] = (acc / l).astype(BF16)


def _full_blocks(i, tq, tk, start, past):
    first_chunk_start = ((start + i * tq) >> CHUNK_SHIFT) << CHUNK_SHIFT
    return (first_chunk_start - (start - past)) // tk


def _check_block_plan(nq, tq, tk, n_keys, start, past):
    for i in range(nq):
        n = _full_blocks(i, tq, tk, start, past)
        assert 0 <= n and (n + 1) * tk <= n_keys, (i, n)
        qc = (start + i * tq + np.arange(tq)) // CHUNK
        kc = (start - past + np.arange(n_keys)) // CHUNK
        vis = kc[None, :] <= qc[:, None]
        assert vis[:, :n * tk].all() and not vis[:, (n + 1) * tk:].any(), (i, n)
        assert vis[:, n * tk:(n + 1) * tk].any(axis=1).all()


def _attention(q, k, v, start, past):
    b, n_heads, s, dq = q.shape
    n_keys = k.shape[2]
    tq = _row_tile(s, 512)
    tk = tq if n_keys % tq == 0 and s > tq else n_keys
    nq = s // tq
    _check_block_plan(nq, tq, tk, n_keys, start, past)
    return pl.pallas_call(
        functools.partial(_attn_kernel, tq=tq, tk=tk, start=start, past=past),
        out_shape=jax.ShapeDtypeStruct((b, s, n_heads * V_HEAD), BF16),
        grid=(b, n_heads, nq),
        in_specs=[pl.BlockSpec((None, None, tq, dq), lambda bi, hi, i: (bi, hi, i, 0)),
                  pl.BlockSpec((None, None, n_keys, dq), lambda bi, hi, i: (bi, hi, 0, 0)),
                  pl.BlockSpec((None, None, n_keys, V_HEAD), lambda bi, hi, i: (bi, hi, 0, 0))],
        out_specs=pl.BlockSpec((None, tq, V_HEAD), lambda bi, hi, i: (bi, i, hi)),
        compiler_params=_params("parallel", "parallel", "arbitrary"),
        name="mla_attn",
    )(q, k, v)


def _pad_lanes(a, width=LANES):
    return jnp.pad(a, [(0, 0)] * (a.ndim - 1) + [(0, width - a.shape[-1])])


_ROPE_SWAP = np.concatenate([np.arange(QK_ROPE // 2, QK_ROPE), np.arange(QK_ROPE // 2)])


def _rope_tables(start, s):
    inv = ROPE_BASE ** (-(jnp.arange(0, QK_ROPE, 2, dtype=F32) / QK_ROPE))
    ang = (start + jnp.arange(s, dtype=jnp.int32)).astype(F32)[:, None] * inv[None, :]
    cos, sin = jnp.cos(ang), jnp.sin(ang)
    return _pad_lanes(jnp.concatenate([cos, cos], axis=-1)), _pad_lanes(jnp.concatenate([-sin, sin], axis=-1))


def _prep_weights(p):
    n_heads = p["w_uk"].shape[1] // QK_NOPE
    w = {"n_heads": n_heads}
    row = lambda a: a.reshape(1, -1).astype(F32)
    for name in ("ffn1_w_in", "ffn1_w_out", "ffn2_w_in", "ffn2_w_out", "pool_w", "w_dq", "w_o"):
        w[name] = p[name].astype(BF16)
    w["w_lat"] = jnp.concatenate(
        [p["w_dkv"], _pad_lanes(p["w_kr"]), _pad_lanes(p["w_kr"][:, _ROPE_SWAP])], axis=1).astype(BF16)
    w["kr_g"] = row(_pad_lanes(p["kr_norm"]))
    w["kr_gs"] = row(_pad_lanes(p["kr_norm"][_ROPE_SWAP]))
    w["w_kv"] = jnp.concatenate([p["w_uk"], p["w_uv"]], axis=1).astype(BF16)
    n_b, q_lora, _ = p["w_uq"].shape
    wq = p["w_uq"].reshape(n_b, q_lora, n_heads, QK_NOPE + QK_ROPE)
    nope, rope = wq[..., :QK_NOPE], wq[..., QK_NOPE:]
    flat = lambda a: a.reshape(n_b, q_lora, -1)
    w["w_q"] = jnp.concatenate(
        [flat(nope), flat(_pad_lanes(rope)), flat(_pad_lanes(rope[..., _ROPE_SWAP]))], axis=-1).astype(BF16)
    w["qr_g"] = _pad_lanes(p["qr_norm"])
    w["qr_gs"] = _pad_lanes(p["qr_norm"][:, _ROPE_SWAP])
    return w


def _trunk(x, pool_prev, ckv_past, kr_past, start, p, w):
    b, s, d = x.shape
    past = 0 if ckv_past is None else ckv_past.shape[1]
    depth = p["ffn1_norm"].shape[0]
    n_a = p["pool_w"].shape[0]
    n_heads = w["n_heads"]
    row = lambda a: a.reshape(1, -1).astype(F32)
    cos, sin = _rope_tables(start, s)
    scale = float((QK_NOPE + QK_ROPE) ** -0.5)
    xf = x.reshape(b * s, d)
    new_pool = []
    for layer in range(depth):
        if layer == n_a:
            c_new, kr_new = _latent(xf, s, row(p["kv_norm"]), w["w_lat"], row(p["c_norm"]), w["kr_g"], w["kr_gs"],
                                    cos, sin)
            c_new = c_new.reshape(b, s, -1)
            kr_new = kr_new.reshape(b, s, LANES)
            c_all, kr_all = c_new, kr_new
            if past:
                c_all = jnp.concatenate([ckv_past.astype(F32), c_new], axis=1)
                kr_all = jnp.concatenate([_pad_lanes(kr_past.astype(F32)), kr_new], axis=1)
            keys, vals = _kv_expand(c_all, kr_all, w["w_kv"], row(p["kn_norm"]), n_heads)
        xf = _ffn(xf, row(p["ffn1_norm"][layer]), w["ffn1_w_in"][layer], w["ffn1_w_out"][layer])
        if layer < n_a:
            prev = jnp.zeros((b, POOL_HIST, d), F32) if pool_prev is None else \
                jnp.pad(pool_prev[layer].astype(F32), ((0, 0), (POOL_HIST - pool_prev.shape[2], 0), (0, 0)))
            xm, st = _pool_mix(xf.reshape(b, s, d), prev, row(p["mix_norm"][layer]), w["pool_w"][layer],
                               row(p["pool_scale"][layer]), start)
            new_pool.append(st[:, 1:])
            xf = _ffn(xm.reshape(b * s, d), row(p["ffn2_norm"][layer]), w["ffn2_w_in"][layer],
                      w["ffn2_w_out"][layer])
        else:
            i = layer - n_a
            q = _q_proj(xf.reshape(b, s, d), row(p["mix_norm"][layer]), w["w_dq"][i], row(p["q_lat_norm"][i]),
                        w["w_q"][i], row(p["qn_norm"][i]), row(w["qr_g"][i]), row(w["qr_gs"][i]), cos, sin,
                        n_heads, scale)
            att = _attention(q, keys, vals, start, past)
            xf = _ffn(xf, row(p["ffn2_norm"][layer]), w["ffn2_w_in"][layer], w["ffn2_w_out"][layer],
                      attn=att.reshape(b * s, -1), w_o=w["w_o"][i])
    return xf.reshape(b, s, d), jnp.stack(new_pool, axis=0), c_new, kr_new[..., :QK_ROPE]


def kernel(x_prompt, x_sample, state_pool, cache_ckv, cache_krope, ffn1_norm, ffn1_w_in, ffn1_w_out, mix_norm, ffn2_norm, ffn2_w_in, ffn2_w_out, pool_w, pool_scale, kv_norm, w_dkv, c_norm, w_kr, kr_norm, w_uk, kn_norm, w_uv, w_dq, q_lat_norm, w_uq, qn_norm, qr_norm, w_o):
    p = dict(ffn1_norm=ffn1_norm, ffn1_w_in=ffn1_w_in, ffn1_w_out=ffn1_w_out, mix_norm=mix_norm,
             ffn2_norm=ffn2_norm, ffn2_w_in=ffn2_w_in, ffn2_w_out=ffn2_w_out, pool_w=pool_w,
             pool_scale=pool_scale, kv_norm=kv_norm, w_dkv=w_dkv, c_norm=c_norm, w_kr=w_kr, kr_norm=kr_norm,
             w_uk=w_uk, kn_norm=kn_norm, w_uv=w_uv, w_dq=w_dq, q_lat_norm=q_lat_norm, w_uq=w_uq,
             qn_norm=qn_norm, qr_norm=qr_norm, w_o=w_o)
    w = _prep_weights(p)
    y_p, pool_p, ckv_p, kr_p = _trunk(x_prompt, None, None, None, 0, p, w)
    y_s, pool_s, ckv_s, kr_s = _trunk(x_sample, state_pool, cache_ckv, cache_krope, cache_ckv.shape[1], p, w)
    return (y_p, y_s, pool_p, pool_s, ckv_p, kr_p, ckv_s, kr_s)
```

```python
import functools

import numpy as np
import jax
import jax.numpy as jnp
from jax import lax
from jax.experimental import pallas as pl
from jax.experimental.pallas import tpu as pltpu

F32 = jnp.float32
BF16 = jnp.bfloat16

EPS = 1e-6
CHUNK = 64
CHUNK_SHIFT = 6
ROPE_BASE = 10000.0
POOL_WINDOWS = (2, 4, 8, 16)
POOL_HIST = 16
QK_NOPE = 128
QK_ROPE = 64
V_HEAD = 128
LANES = 128
VMEM_LIMIT_BYTES = 56 * 1024 * 1024
FFN_ROWS = 512


def _rms(x, g, n=None):
    n = x.shape[-1] if n is None else n
    ms = jnp.sum(x * x, axis=-1, keepdims=True) * (1.0 / n)
    return x * lax.rsqrt(ms + EPS) * g


def _resident(shape):
    zeros = (0,) * len(shape)
    return pl.BlockSpec(shape, lambda *_: zeros, pipeline_mode=pl.Buffered(1))


def _params(*sem):
    return pltpu.CompilerParams(dimension_semantics=sem, vmem_limit_bytes=VMEM_LIMIT_BYTES)


def _row_tile(t, cap=FFN_ROWS):
    tm = cap
    while t % tm:
        tm //= 2
    assert tm >= 8, (t, cap)
    return tm


def _ffn_body(x, g_ref, win_ref, wout_ref, d_ff):
    h = _rms(x, g_ref[...]).astype(BF16)
    gu = jnp.dot(h, win_ref[...], preferred_element_type=F32)
    a = (jax.nn.silu(gu[:, :d_ff]) * gu[:, d_ff:]).astype(BF16)
    return x + 0.5 * jnp.dot(a, wout_ref[...], preferred_element_type=F32)


def _ffn_kernel(x_ref, g_ref, win_ref, wout_ref, o_ref, *, d_ff):
    o_ref[...] = _ffn_body(x_ref[...], g_ref, win_ref, wout_ref, d_ff)


def _ffn_wo_kernel(x_ref, a_ref, wo_ref, g_ref, win_ref, wout_ref, o_ref, *, d_ff):
    x = x_ref[...] + jnp.dot(a_ref[...], wo_ref[...], preferred_element_type=F32)
    o_ref[...] = _ffn_body(x, g_ref, win_ref, wout_ref, d_ff)


def _ffn(x, g, w_in, w_out, attn=None, w_o=None):
    t, d = x.shape
    d_ff = w_out.shape[0]
    tm = _row_tile(t)
    row = pl.BlockSpec((tm, d), lambda i: (i, 0))
    w_specs = [_resident((1, d)), _resident(w_in.shape), _resident(w_out.shape)]
    if attn is None:
        kern, ins, specs = _ffn_kernel, (x, g, w_in, w_out), [row] + w_specs
    else:
        kern = _ffn_wo_kernel
        ins = (x, attn, w_o, g, w_in, w_out)
        specs = [row, pl.BlockSpec((tm, attn.shape[1]), lambda i: (i, 0)), _resident(w_o.shape)] + w_specs
    return pl.pallas_call(
        functools.partial(kern, d_ff=d_ff),
        out_shape=jax.ShapeDtypeStruct((t, d), F32),
        grid=(t // tm,),
        in_specs=specs,
        out_specs=row,
        compiler_params=_params("parallel"),
        name="ffn" if attn is None else "ffn_wo",
    )(*ins)


def _pool_body(x_ref, prev_ref, g_ref, w_ref, sc_ref, st_ref, carry_ref, start, ts):
    j = pl.program_id(1)

    @pl.when(j == 0)
    def _():
        prev = prev_ref[...]
        if start < POOL_HIST:
            rowpos = start - POOL_HIST + lax.broadcasted_iota(jnp.int32, (POOL_HIST, 1), 0)
            prev = jnp.where(rowpos >= 0, prev, 0.0)
        carry_ref[...] = prev

    x = x_ref[...]
    h = _rms(x, g_ref[...])
    ext = jnp.concatenate([carry_ref[...], h], axis=0)
    carry_ref[...] = h[ts - POOL_HIST:, :]
    st_ref[...] = h[ts - POOL_HIST:, :]
    pos = start + j * ts + lax.broadcasted_iota(jnp.int32, (ts, 1), 0)
    group = w_ref.shape[1]
    outs = []
    for gi, w in enumerate(POOL_WINDOWS):
        lo, hi = gi * group, (gi + 1) * group
        p = ext[:, lo:hi]
        k = 1
        while k < w:
            p = p + pltpu.roll(p, k, axis=0)
            k *= 2
        cnt = jnp.minimum(pos + 1, w).astype(F32)
        dlt = p[POOL_HIST:, :] / cnt - h[:, lo:hi]
        outs.append(jnp.dot(dlt.astype(BF16), w_ref[gi], preferred_element_type=F32))
    return x + jnp.concatenate(outs, axis=-1) * sc_ref[...]


def _pool_kernel(x_ref, prev_ref, g_ref, w_ref, sc_ref, o_ref, st_ref, carry_ref, *, start, ts):
    o_ref[...] = _pool_body(x_ref, prev_ref, g_ref, w_ref, sc_ref, st_ref, carry_ref, start, ts)


def _pool_ffn_kernel(x_ref, prev_ref, g_ref, w_ref, sc_ref, fg_ref, win_ref, wout_ref, o_ref, st_ref, carry_ref,
                     *, start, ts, d_ff):
    xm = _pool_body(x_ref, prev_ref, g_ref, w_ref, sc_ref, st_ref, carry_ref, start, ts)
    o_ref[...] = _ffn_body(xm, fg_ref, win_ref, wout_ref, d_ff)


def _pool_mix(x, prev16, g, w_pool, scale, start, ffn=None):
    b, s, d = x.shape
    ts = _row_tile(s)
    assert ts >= POOL_HIST and len(POOL_WINDOWS) * w_pool.shape[1] == d
    seq = pl.BlockSpec((None, ts, d), lambda bi, j: (bi, j, 0))
    hist = pl.BlockSpec((None, POOL_HIST, d), lambda bi, j: (bi, 0, 0))
    ins = [x, prev16, g, w_pool, scale]
    specs = [seq, hist, _resident((1, d)), _resident(w_pool.shape), _resident((1, d))]
    if ffn is None:
        kern = functools.partial(_pool_kernel, start=start, ts=ts)
    else:
        kern = functools.partial(_pool_ffn_kernel, start=start, ts=ts, d_ff=ffn[2].shape[0])
        ins += list(ffn)
        specs += [_resident((1, d)), _resident(ffn[1].shape), _resident(ffn[2].shape)]
    return pl.pallas_call(
        kern,
        out_shape=(jax.ShapeDtypeStruct((b, s, d), F32), jax.ShapeDtypeStruct((b, POOL_HIST, d), F32)),
        grid=(b, s // ts),
        in_specs=specs,
        out_specs=(seq, hist),
        scratch_shapes=[pltpu.VMEM((POOL_HIST, d), F32)],
        compiler_params=_params("parallel", "arbitrary"),
        name="pool_mix" if ffn is None else "pool_ffn",
    )(*ins)


def _rope_rows(z, z_sw, g, g_sw, cos, sin, same_halves):
    zz = z * z
    if same_halves:
        r = lax.rsqrt(jnp.sum(zz, axis=-1, keepdims=True) * (1.0 / LANES) + EPS)
    else:
        lo = lax.broadcasted_iota(jnp.int32, (1, LANES), 1) < QK_ROPE
        r_lo = lax.rsqrt(jnp.sum(jnp.where(lo, zz, 0.0), axis=-1, keepdims=True) * (1.0 / QK_ROPE) + EPS)
        r_hi = lax.rsqrt(jnp.sum(jnp.where(lo, 0.0, zz), axis=-1, keepdims=True) * (1.0 / QK_ROPE) + EPS)
        r = jnp.where(lo, r_lo, r_hi)
    return (z * r) * g * cos + (z_sw * r) * g_sw * sin


def _latent_kernel(x_ref, g_ref, w_ref, cg_ref, rg_ref, rgs_ref, cos_ref, sin_ref, c_ref, kr_ref, *, kv_lora):
    h = _rms(x_ref[...], g_ref[...]).astype(BF16)
    z = jnp.dot(h, w_ref[...], preferred_element_type=F32)
    c_ref[...] = _rms(z[:, :kv_lora], cg_ref[...])
    kr_ref[...] = _rope_rows(z[:, kv_lora:kv_lora + LANES], z[:, kv_lora + LANES:],
                             rg_ref[...], rgs_ref[...], cos_ref[...], sin_ref[...], same_halves=True)


def _table_spec(tm, s):
    if tm <= s:
        assert s % tm == 0
        per = s // tm
        return pl.BlockSpec((tm, LANES), lambda i: (i % per, 0))
    return pl.BlockSpec((tm, LANES), lambda i: (0, 0))


def _tables_for(cos, sin, tm, s):
    if tm <= s:
        return cos, sin
    assert tm % s == 0
    return jnp.tile(cos, (tm // s, 1)), jnp.tile(sin, (tm // s, 1))


def _latent(x, s, g, w_lat, c_g, r_g, r_gs, cos, sin):
    t, d = x.shape
    kv_lora = c_g.shape[1]
    tm = _row_tile(t)
    cos_t, sin_t = _tables_for(cos, sin, tm, s)
    return pl.pallas_call(
        functools.partial(_latent_kernel, kv_lora=kv_lora),
        out_shape=(jax.ShapeDtypeStruct((t, kv_lora), F32), jax.ShapeDtypeStruct((t, LANES), F32)),
        grid=(t // tm,),
        in_specs=[pl.BlockSpec((tm, d), lambda i: (i, 0)), _resident((1, d)), _resident(w_lat.shape),
                  _resident((1, kv_lora)), _resident((1, LANES)), _resident((1, LANES)),
                  _table_spec(tm, s), _table_spec(tm, s)],
        out_specs=(pl.BlockSpec((tm, kv_lora), lambda i: (i, 0)), pl.BlockSpec((tm, LANES), lambda i: (i, 0))),
        compiler_params=_params("parallel"),
        name="mla_latent",
    )(x, g, w_lat, c_g, r_g, r_gs, cos_t, sin_t)


def _kv_kernel(c_ref, kr_ref, w_ref, g_ref, k_ref, v_ref, *, n_heads):
    z = jnp.dot(c_ref[...].astype(BF16), w_ref[...], preferred_element_type=F32)
    lo = lax.broadcasted_iota(jnp.int32, (1, LANES), 1) < QK_ROPE
    kr = kr_ref[...]
    kr_half = (jnp.where(lo, kr, 0.0).astype(BF16), jnp.where(lo, 0.0, kr).astype(BF16))
    for hh in range(n_heads):
        kn = _rms(z[:, hh * QK_NOPE:(hh + 1) * QK_NOPE], g_ref[...]).astype(BF16)
        k_ref[hh] = jnp.concatenate([kn, kr_half[hh % 2]], axis=-1)
        off = n_heads * QK_NOPE + hh * V_HEAD
        v_ref[hh] = z[:, off:off + V_HEAD].astype(BF16)


def _kv_expand(c_all, kr_all, w_kv, kn_g, n_heads):
    b, tk, kv_lora = c_all.shape
    tr = tk if tk % 512 else 512
    return pl.pallas_call(
        functools.partial(_kv_kernel, n_heads=n_heads),
        out_shape=(jax.ShapeDtypeStruct((b, n_heads, tk, QK_NOPE + LANES), BF16),
                   jax.ShapeDtypeStruct((b, n_heads, tk, V_HEAD), BF16)),
        grid=(b, tk // tr),
        in_specs=[pl.BlockSpec((None, tr, kv_lora), lambda bi, j: (bi, j, 0)),
                  pl.BlockSpec((None, tr, LANES), lambda bi, j: (bi, j, 0)),
                  _resident(w_kv.shape), _resident((1, QK_NOPE))],
        out_specs=(pl.BlockSpec((None, n_heads, tr, QK_NOPE + LANES), lambda bi, j: (bi, 0, j, 0)),
                   pl.BlockSpec((None, n_heads, tr, V_HEAD), lambda bi, j: (bi, 0, j, 0))),
        compiler_params=_params("parallel", "parallel"),
        name="mla_kv",
    )(c_all, kr_all, w_kv, kn_g)


def _q_kernel(x_ref, g_ref, wdq_ref, lg_ref, wq_ref, ng_ref, rg_ref, rgs_ref, cos_ref, sin_ref, q_ref,
              *, n_heads, scale):
    h = _rms(x_ref[...], g_ref[...]).astype(BF16)
    ql = _rms(jnp.dot(h, wdq_ref[...], preferred_element_type=F32), lg_ref[...]).astype(BF16)
    z = jnp.dot(ql, wq_ref[...], preferred_element_type=F32)
    cos, sin = cos_ref[...], sin_ref[...]
    n_pairs = n_heads // 2
    for pp in range(n_pairs):
        qn = [_rms(z[:, hh * QK_NOPE:(hh + 1) * QK_NOPE], ng_ref[...]) * scale for hh in (2 * pp, 2 * pp + 1)]
        o1 = n_heads * QK_NOPE + pp * LANES
        o2 = o1 + n_pairs * LANES
        qr = _rope_rows(z[:, o1:o1 + LANES], z[:, o2:o2 + LANES], rg_ref[...], rgs_ref[...], cos, sin,
                        same_halves=False)
        q_ref[pp] = jnp.concatenate(qn + [qr * scale], axis=-1).astype(BF16)


def _q_proj(x, g, w_dq, l_g, w_q, n_g, r_g, r_gs, cos, sin, n_heads, scale):
    b, s, d = x.shape
    tm = _row_tile(s)
    q_lora = w_dq.shape[1]
    n_pairs = n_heads // 2
    dq = 2 * QK_NOPE + LANES
    tab = pl.BlockSpec((tm, LANES), lambda bi, j: (j, 0))
    return pl.pallas_call(
        functools.partial(_q_kernel, n_heads=n_heads, scale=scale),
        out_shape=jax.ShapeDtypeStruct((b, n_pairs, s, dq), BF16),
        grid=(b, s // tm),
        in_specs=[pl.BlockSpec((None, tm, d), lambda bi, j: (bi, j, 0)), _resident((1, d)),
                  _resident(w_dq.shape), _resident((1, q_lora)), _resident(w_q.shape),
                  _resident((1, QK_NOPE)), _resident((1, LANES)), _resident((1, LANES)), tab, tab],
        out_specs=pl.BlockSpec((None, n_pairs, tm, dq), lambda bi, j: (bi, 0, j, 0)),
        compiler_params=_params("parallel", "parallel"),
        name="mla_q",
    )(x, g, w_dq, l_g, w_q, n_g, r_g, r_gs, cos, sin)


def _attn_kernel(q_ref, k_ref, v_ref, o_ref, *, hg, tq, tk, nq, start, past):
    def attend(ti):
        nk = (_full_blocks(ti, tq, tk, start, past) + 1) * tk
        qpos = start + ti * tq + lax.broadcasted_iota(jnp.int32, (tq, 1), 0)
        kpos = start - past + (nk - tk) + lax.broadcasted_iota(jnp.int32, (1, tk), 1)
        visible = jnp.right_shift(kpos, CHUNK_SHIFT) <= jnp.right_shift(qpos, CHUNK_SHIFT)
        scores, probs, outs = [], [], []
        for hh in range(hg):
            lo = (hh % 2) * QK_NOPE
            q = jnp.concatenate([q_ref[hh // 2, :, lo:lo + QK_NOPE], q_ref[hh // 2, :, 2 * QK_NOPE:]], axis=-1)
            scores.append(lax.dot_general(q, k_ref[hh, :nk, :], (((1,), (1,)), ((), ())),
                                          preferred_element_type=F32))
        for s in scores:
            last = jnp.where(visible, s[:, nk - tk:], -jnp.inf)
            m = jnp.max(last, axis=-1, keepdims=True)
            if nk > tk:
                m = jnp.maximum(m, jnp.max(s[:, :nk - tk], axis=-1, keepdims=True))
                p = jnp.concatenate([jnp.exp2(s[:, :nk - tk] - m), jnp.exp2(last - m)], axis=-1)
            else:
                p = jnp.exp2(last - m)
            probs.append((p.astype(BF16), jnp.sum(p, axis=-1, keepdims=True)))
        for hh, (p, l) in enumerate(probs):
            outs.append(jnp.dot(p, v_ref[hh, :nk, :], preferred_element_type=F32) / l)
        o_ref[...] = jnp.concatenate(outs, axis=-1).astype(BF16)

    if nq == 1:
        attend(0)
    else:
        i = pl.program_id(2)
        for ti in range(nq):
            pl.when(i == ti)(functools.partial(attend, ti))


def _full_blocks(i, tq, tk, start, past):
    first_chunk_start = ((start + i * tq) >> CHUNK_SHIFT) << CHUNK_SHIFT
    return (first_chunk_start - (start - past)) // tk


def _check_block_plan(nq, tq, tk, n_keys, start, past):
    for i in range(nq):
        n = _full_blocks(i, tq, tk, start, past)
        assert 0 <= n and (n + 1) * tk <= n_keys, (i, n)
        qc = (start + i * tq + np.arange(tq)) // CHUNK
        kc = (start - past + np.arange(n_keys)) // CHUNK
        vis = kc[None, :] <= qc[:, None]
        assert vis[:, :n * tk].all() and not vis[:, (n + 1) * tk:].any(), (i, n)
        assert vis[:, n * tk:(n + 1) * tk].any(axis=1).all()


def _attention(q, k, v, start, past):
    b, n_pairs, s, dq = q.shape
    n_heads, n_keys, dk = k.shape[1:]
    assert n_heads == 2 * n_pairs
    tq = _row_tile(s)
    tk = tq if n_keys % tq == 0 and s > tq else n_keys
    nq = s // tq
    hg = 4 if n_heads % 4 == 0 else 2
    _check_block_plan(nq, tq, tk, n_keys, start, past)
    return pl.pallas_call(
        functools.partial(_attn_kernel, hg=hg, tq=tq, tk=tk, nq=nq, start=start, past=past),
        out_shape=jax.ShapeDtypeStruct((b, s, n_heads * V_HEAD), BF16),
        grid=(b, n_heads // hg, nq),
        in_specs=[pl.BlockSpec((None, hg // 2, tq, dq), lambda bi, gi, i: (bi, gi, i, 0)),
                  pl.BlockSpec((None, hg, n_keys, dk), lambda bi, gi, i: (bi, gi, 0, 0)),
                  pl.BlockSpec((None, hg, n_keys, V_HEAD), lambda bi, gi, i: (bi, gi, 0, 0))],
        out_specs=pl.BlockSpec((None, tq, hg * V_HEAD), lambda bi, gi, i: (bi, i, gi)),
        compiler_params=_params("parallel", "parallel", "arbitrary"),
        name="mla_attn",
    )(q, k, v)


def _twice(a):
    return jnp.concatenate([a, a], axis=-1)


_ROPE_SWAP = np.concatenate([np.arange(QK_ROPE // 2, QK_ROPE), np.arange(QK_ROPE // 2)])


def _rope_tables(start, s):
    inv = ROPE_BASE ** (-(jnp.arange(0, QK_ROPE, 2, dtype=F32) / QK_ROPE))
    ang = (start + jnp.arange(s, dtype=jnp.int32)).astype(F32)[:, None] * inv[None, :]
    cos, sin = jnp.cos(ang), jnp.sin(ang)
    return _twice(jnp.concatenate([cos, cos], axis=-1)), _twice(jnp.concatenate([-sin, sin], axis=-1))


def _prep_weights(p):
    n_heads = p["w_uk"].shape[1] // QK_NOPE
    assert n_heads % 2 == 0
    w = {"n_heads": n_heads}
    row = lambda a: a.reshape(1, -1).astype(F32)
    for name in ("ffn1_w_in", "ffn1_w_out", "ffn2_w_in", "ffn2_w_out", "pool_w", "w_dq", "w_o"):
        w[name] = p[name].astype(BF16)
    w["w_lat"] = jnp.concatenate(
        [p["w_dkv"], _twice(p["w_kr"]), _twice(p["w_kr"][:, _ROPE_SWAP])], axis=1).astype(BF16)
    w["kr_g"] = row(_twice(p["kr_norm"]))
    w["kr_gs"] = row(_twice(p["kr_norm"][_ROPE_SWAP]))
    w["w_kv"] = jnp.concatenate([p["w_uk"], p["w_uv"]], axis=1).astype(BF16)
    n_b, q_lora, _ = p["w_uq"].shape
    wq = p["w_uq"].reshape(n_b, q_lora, n_heads, QK_NOPE + QK_ROPE)
    nope, rope = wq[..., :QK_NOPE], wq[..., QK_NOPE:]
    flat = lambda a: a.reshape(n_b, q_lora, -1)
    w["w_q"] = jnp.concatenate([flat(nope), flat(rope), flat(rope[..., _ROPE_SWAP])], axis=-1).astype(BF16)
    w["qr_g"] = _twice(p["qr_norm"])
    w["qr_gs"] = _twice(p["qr_norm"][:, _ROPE_SWAP])
    return w


def _trunk(x, pool_prev, ckv_past, kr_past, start, p, w):
    b, s, d = x.shape
    past = 0 if ckv_past is None else ckv_past.shape[1]
    depth = p["ffn1_norm"].shape[0]
    n_a = p["pool_w"].shape[0]
    n_heads = w["n_heads"]
    row = lambda a: a.reshape(1, -1).astype(F32)
    cos, sin = _rope_tables(start, s)
    scale = float((QK_NOPE + QK_ROPE) ** -0.5 * np.log2(np.e))
    xf = x.reshape(b * s, d)
    new_pool = []
    for layer in range(depth):
        if layer == n_a:
            c_new, kr_new = _latent(xf, s, row(p["kv_norm"]), w["w_lat"], row(p["c_norm"]), w["kr_g"], w["kr_gs"],
                                    cos, sin)
            c_new = c_new.reshape(b, s, -1)
            kr_new = kr_new.reshape(b, s, LANES)
            c_all, kr_all = c_new, kr_new
            if past:
                c_all = jnp.concatenate([ckv_past.astype(F32), c_new], axis=1)
                kr_all = jnp.concatenate([_twice(kr_past.astype(F32)), kr_new], axis=1)
            keys, vals = _kv_expand(c_all, kr_all, w["w_kv"], row(p["kn_norm"]), n_heads)
        xf = _ffn(xf, row(p["ffn1_norm"][layer]), w["ffn1_w_in"][layer], w["ffn1_w_out"][layer])
        if layer < n_a:
            prev = jnp.zeros((b, POOL_HIST, d), F32) if pool_prev is None else \
                jnp.pad(pool_prev[layer].astype(F32), ((0, 0), (POOL_HIST - pool_prev.shape[2], 0), (0, 0)))
            ffn2 = (row(p["ffn2_norm"][layer]), w["ffn2_w_in"][layer], w["ffn2_w_out"][layer])
            fuse = s % FFN_ROWS == 0
            xm, st = _pool_mix(xf.reshape(b, s, d), prev, row(p["mix_norm"][layer]), w["pool_w"][layer],
                               row(p["pool_scale"][layer]), start, ffn=ffn2 if fuse else None)
            new_pool.append(st[:, 1:])
            xf = xm.reshape(b * s, d) if fuse else _ffn(xm.reshape(b * s, d), *ffn2)
        else:
            i = layer - n_a
            q = _q_proj(xf.reshape(b, s, d), row(p["mix_norm"][layer]), w["w_dq"][i], row(p["q_lat_norm"][i]),
                        w["w_q"][i], row(p["qn_norm"][i]), row(w["qr_g"][i]), row(w["qr_gs"][i]), cos, sin,
                        n_heads, scale)
            att = _attention(q, keys, vals, start, past)
            xf = _ffn(xf, row(p["ffn2_norm"][layer]), w["ffn2_w_in"][layer], w["ffn2_w_out"][layer],
                      attn=att.reshape(b * s, -1), w_o=w["w_o"][i])
    return xf.reshape(b, s, d), jnp.stack(new_pool, axis=0), c_new, kr_new[..., :QK_ROPE]


def kernel(x_prompt, x_sample, state_pool, cache_ckv, cache_krope, ffn1_norm, ffn1_w_in, ffn1_w_out, mix_norm, ffn2_norm, ffn2_w_in, ffn2_w_out, pool_w, pool_scale, kv_norm, w_dkv, c_norm, w_kr, kr_norm, w_uk, kn_norm, w_uv, w_dq, q_lat_norm, w_uq, qn_norm, qr_norm, w_o):
    p = dict(ffn1_norm=ffn1_norm, ffn1_w_in=ffn1_w_in, ffn1_w_out=ffn1_w_out, mix_norm=mix_norm,
             ffn2_norm=ffn2_norm, ffn2_w_in=ffn2_w_in, ffn2_w_out=ffn2_w_out, pool_w=pool_w,
             pool_scale=pool_scale, kv_norm=kv_norm, w_dkv=w_dkv, c_norm=c_norm, w_kr=w_kr, kr_norm=kr_norm,
             w_uk=w_uk, kn_norm=kn_norm, w_uv=w_uv, w_dq=w_dq, q_lat_norm=q_lat_norm, w_uq=w_uq,
             qn_norm=qn_norm, qr_norm=qr_norm, w_o=w_o)
    w = _prep_weights(p)
    y_p, pool_p, ckv_p, kr_p = _trunk(x_prompt, None, None, None, 0, p, w)
    y_s, pool_s, ckv_s, kr_s = _trunk(x_sample, state_pool, cache_ckv, cache_krope, cache_ckv.shape[1], p, w)
    return (y_p, y_s, pool_p, pool_s, ckv_p, kr_p, ckv_s, kr_s)
```

```python
import functools
from typing import NamedTuple

import numpy as np
import jax
import jax.numpy as jnp
from jax import lax
from jax.experimental import pallas as pl
from jax.experimental.pallas import tpu as pltpu

F32 = jnp.float32
BF16 = jnp.bfloat16

EPS = 1e-6
CHUNK = 64
CHUNK_SHIFT = 6
ROPE_BASE = 10000.0
POOL_WINDOWS = (2, 4, 8, 16)
POOL_HIST = 16
QK_NOPE = 128
QK_ROPE = 64
V_HEAD = 128
LANES = 128
VMEM_LIMIT_BYTES = 56 * 1024 * 1024
FFN_ROWS = 512
STREAM_ROWS = 1024


def _rms(x, g, n=None):
    n = x.shape[-1] if n is None else n
    ms = jnp.sum(x * x, axis=-1, keepdims=True) * (1.0 / n)
    return x * lax.rsqrt(ms + EPS) * g


class _Stacked(NamedTuple):
    stack: jax.Array
    idx: int

    @property
    def shape(self):
        return self.stack.shape[1:]


def _arr(a):
    return a.stack if isinstance(a, _Stacked) else a


def _resident(a):
    shape = a if isinstance(a, tuple) and not isinstance(a, _Stacked) else tuple(a.shape)
    zeros = (0,) * len(shape)
    if isinstance(a, _Stacked):
        return pl.BlockSpec((None,) + shape, lambda *_: (a.idx,) + zeros, pipeline_mode=pl.Buffered(1))
    return pl.BlockSpec(shape, lambda *_: zeros, pipeline_mode=pl.Buffered(1))


def _params(*sem):
    return pltpu.CompilerParams(dimension_semantics=sem, vmem_limit_bytes=VMEM_LIMIT_BYTES)


def _row_tile(t, cap=FFN_ROWS):
    tm = cap
    while t % tm:
        tm //= 2
    assert tm >= 8, (t, cap)
    return tm


def _ffn_body(x, g_ref, win_ref, wout_ref, d_ff):
    h = _rms(x, g_ref[...]).astype(BF16)
    gu = jnp.dot(h, win_ref[...], preferred_element_type=F32)
    a = (jax.nn.silu(gu[:, :d_ff]) * gu[:, d_ff:]).astype(BF16)
    return x + 0.5 * jnp.dot(a, wout_ref[...], preferred_element_type=F32)


def _ffn_kernel(x_ref, g_ref, win_ref, wout_ref, o_ref, *, d_ff):
    o_ref[...] = _ffn_body(x_ref[...], g_ref, win_ref, wout_ref, d_ff)


def _ffn_wo_kernel(x_ref, a_ref, wo_ref, g_ref, win_ref, wout_ref, o_ref, *, d_ff):
    x = x_ref[...] + jnp.dot(a_ref[...], wo_ref[...], preferred_element_type=F32)
    o_ref[...] = _ffn_body(x, g_ref, win_ref, wout_ref, d_ff)


def _ffn(x, g, w_in, w_out, attn=None, w_o=None):
    t, d = x.shape
    d_ff = w_out.shape[0]
    tm = _row_tile(t)
    row = pl.BlockSpec((tm, d), lambda i: (i, 0))
    w_specs = [_resident((1, d)), _resident(w_in), _resident(w_out)]
    if attn is None:
        kern, ins, specs = _ffn_kernel, (x, g, w_in, w_out), [row] + w_specs
    else:
        kern = _ffn_wo_kernel
        ins = (x, attn, w_o, g, w_in, w_out)
        specs = [row, pl.BlockSpec((tm, attn.shape[1]), lambda i: (i, 0)), _resident(w_o)] + w_specs
    return pl.pallas_call(
        functools.partial(kern, d_ff=d_ff),
        out_shape=jax.ShapeDtypeStruct((t, d), F32),
        grid=(t // tm,),
        in_specs=specs,
        out_specs=row,
        compiler_params=_params("parallel"),
        name="ffn" if attn is None else "ffn_wo",
    )(*map(_arr, ins))


def _pool_body(x_ref, prev_ref, g_ref, w_ref, sc_ref, st_ref, carry_ref, start, ts):
    j = pl.program_id(1)

    @pl.when(j == 0)
    def _():
        prev = prev_ref[...]
        if start < POOL_HIST:
            rowpos = start - POOL_HIST + lax.broadcasted_iota(jnp.int32, (POOL_HIST, 1), 0)
            prev = jnp.where(rowpos >= 0, prev, 0.0)
        carry_ref[...] = prev

    x = x_ref[...]
    h = _rms(x, g_ref[...])
    ext = jnp.concatenate([carry_ref[...], h], axis=0)
    carry_ref[...] = h[ts - POOL_HIST:, :]
    st_ref[...] = h[ts - POOL_HIST:, :]
    pos = start + j * ts + lax.broadcasted_iota(jnp.int32, (ts, 1), 0)
    group = w_ref.shape[1]
    outs = []
    for gi, w in enumerate(POOL_WINDOWS):
        lo, hi = gi * group, (gi + 1) * group
        p = ext[:, lo:hi]
        k = 1
        while k < w:
            p = p + pltpu.roll(p, k, axis=0)
            k *= 2
        cnt = jnp.minimum(pos + 1, w).astype(F32)
        dlt = p[POOL_HIST:, :] / cnt - h[:, lo:hi]
        outs.append(jnp.dot(dlt.astype(BF16), w_ref[gi], preferred_element_type=F32))
    return x + jnp.concatenate(outs, axis=-1) * sc_ref[...]


def _pool_kernel(x_ref, prev_ref, g_ref, w_ref, sc_ref, o_ref, st_ref, carry_ref, *, start, ts):
    o_ref[...] = _pool_body(x_ref, prev_ref, g_ref, w_ref, sc_ref, st_ref, carry_ref, start, ts)


def _pool_ffn_kernel(x_ref, prev_ref, g_ref, w_ref, sc_ref, fg_ref, win_ref, wout_ref, o_ref, st_ref, carry_ref,
                     *, start, ts, d_ff):
    xm = _pool_body(x_ref, prev_ref, g_ref, w_ref, sc_ref, st_ref, carry_ref, start, ts)
    o_ref[...] = _ffn_body(xm, fg_ref, win_ref, wout_ref, d_ff)


def _pool_mix(x, prev16, g, w_pool, scale, start, ffn=None):
    b, s, d = x.shape
    ts = _row_tile(s)
    assert ts >= POOL_HIST and len(POOL_WINDOWS) * w_pool.shape[1] == d
    seq = pl.BlockSpec((None, ts, d), lambda bi, j: (bi, j, 0))
    hist = pl.BlockSpec((None, POOL_HIST, d), lambda bi, j: (bi, 0, 0))
    ins = [x, prev16, g, w_pool, scale]
    specs = [seq, hist, _resident((1, d)), _resident(w_pool), _resident((1, d))]
    if ffn is None:
        kern = functools.partial(_pool_kernel, start=start, ts=ts)
    else:
        kern = functools.partial(_pool_ffn_kernel, start=start, ts=ts, d_ff=ffn[2].shape[0])
        ins += list(ffn)
        specs += [_resident((1, d)), _resident(ffn[1]), _resident(ffn[2])]
    return pl.pallas_call(
        kern,
        out_shape=(jax.ShapeDtypeStruct((b, s, d), F32), jax.ShapeDtypeStruct((b, POOL_HIST, d), F32)),
        grid=(b, s // ts),
        in_specs=specs,
        out_specs=(seq, hist),
        scratch_shapes=[pltpu.VMEM((POOL_HIST, d), F32)],
        compiler_params=_params("parallel", "arbitrary"),
        name="pool_mix" if ffn is None else "pool_ffn",
    )(*map(_arr, ins))


def _rope_rows(z, z_sw, g, g_sw, cos, sin, same_halves):
    zz = z * z
    if same_halves:
        r = lax.rsqrt(jnp.sum(zz, axis=-1, keepdims=True) * (1.0 / LANES) + EPS)
    else:
        lo = lax.broadcasted_iota(jnp.int32, (1, LANES), 1) < QK_ROPE
        r_lo = lax.rsqrt(jnp.sum(jnp.where(lo, zz, 0.0), axis=-1, keepdims=True) * (1.0 / QK_ROPE) + EPS)
        r_hi = lax.rsqrt(jnp.sum(jnp.where(lo, 0.0, zz), axis=-1, keepdims=True) * (1.0 / QK_ROPE) + EPS)
        r = jnp.where(lo, r_lo, r_hi)
    return (z * r) * g * cos + (z_sw * r) * g_sw * sin


def _latent_kernel(x_ref, g_ref, w_ref, cg_ref, rg_ref, rgs_ref, cos_ref, sin_ref, c_ref, kr_ref, *, kv_lora):
    h = _rms(x_ref[...], g_ref[...]).astype(BF16)
    z = jnp.dot(h, w_ref[...], preferred_element_type=F32)
    c_ref[...] = _rms(z[:, :kv_lora], cg_ref[...])
    kr_ref[...] = _rope_rows(z[:, kv_lora:kv_lora + LANES], z[:, kv_lora + LANES:],
                             rg_ref[...], rgs_ref[...], cos_ref[...], sin_ref[...], same_halves=True)


def _table_spec(tm, s):
    if tm <= s:
        assert s % tm == 0
        per = s // tm
        return pl.BlockSpec((tm, LANES), lambda i: (i % per, 0))
    return pl.BlockSpec((tm, LANES), lambda i: (0, 0))


def _tables_for(cos, sin, tm, s):
    if tm <= s:
        return cos, sin
    assert tm % s == 0
    return jnp.tile(cos, (tm // s, 1)), jnp.tile(sin, (tm // s, 1))


def _latent(x, s, g, w_lat, c_g, r_g, r_gs, cos, sin):
    t, d = x.shape
    kv_lora = c_g.shape[1]
    tm = _row_tile(t, STREAM_ROWS)
    cos_t, sin_t = _tables_for(cos, sin, tm, s)
    return pl.pallas_call(
        functools.partial(_latent_kernel, kv_lora=kv_lora),
        out_shape=(jax.ShapeDtypeStruct((t, kv_lora), F32), jax.ShapeDtypeStruct((t, LANES), F32)),
        grid=(t // tm,),
        in_specs=[pl.BlockSpec((tm, d), lambda i: (i, 0)), _resident((1, d)), _resident(w_lat.shape),
                  _resident((1, kv_lora)), _resident((1, LANES)), _resident((1, LANES)),
                  _table_spec(tm, s), _table_spec(tm, s)],
        out_specs=(pl.BlockSpec((tm, kv_lora), lambda i: (i, 0)), pl.BlockSpec((tm, LANES), lambda i: (i, 0))),
        compiler_params=_params("parallel"),
        name="mla_latent",
    )(x, g, w_lat, c_g, r_g, r_gs, cos_t, sin_t)


def _kv_kernel(c_ref, kr_ref, w_ref, g_ref, k_ref, v_ref, *, n_heads):
    z = jnp.dot(c_ref[...].astype(BF16), w_ref[...], preferred_element_type=F32)
    lo = lax.broadcasted_iota(jnp.int32, (1, LANES), 1) < QK_ROPE
    kr = kr_ref[...]
    kr_half = (jnp.where(lo, kr, 0.0).astype(BF16), jnp.where(lo, 0.0, kr).astype(BF16))
    for hh in range(n_heads):
        kn = _rms(z[:, hh * QK_NOPE:(hh + 1) * QK_NOPE], g_ref[...]).astype(BF16)
        k_ref[hh] = jnp.concatenate([kn, kr_half[hh % 2]], axis=-1)
        off = n_heads * QK_NOPE + hh * V_HEAD
        v_ref[hh] = z[:, off:off + V_HEAD].astype(BF16)


def _kv_expand(c_all, kr_all, w_kv, kn_g, n_heads):
    b, tk, kv_lora = c_all.shape
    tr = tk if tk % STREAM_ROWS else STREAM_ROWS
    return pl.pallas_call(
        functools.partial(_kv_kernel, n_heads=n_heads),
        out_shape=(jax.ShapeDtypeStruct((b, n_heads, tk, QK_NOPE + LANES), BF16),
                   jax.ShapeDtypeStruct((b, n_heads, tk, V_HEAD), BF16)),
        grid=(b, tk // tr),
        in_specs=[pl.BlockSpec((None, tr, kv_lora), lambda bi, j: (bi, j, 0)),
                  pl.BlockSpec((None, tr, LANES), lambda bi, j: (bi, j, 0)),
                  _resident(w_kv.shape), _resident((1, QK_NOPE))],
        out_specs=(pl.BlockSpec((None, n_heads, tr, QK_NOPE + LANES), lambda bi, j: (bi, 0, j, 0)),
                   pl.BlockSpec((None, n_heads, tr, V_HEAD), lambda bi, j: (bi, 0, j, 0))),
        compiler_params=_params("parallel", "parallel"),
        name="mla_kv",
    )(c_all, kr_all, w_kv, kn_g)


def _q_kernel(x_ref, g_ref, wdq_ref, lg_ref, wq_ref, ng_ref, rg_ref, rgs_ref, cos_ref, sin_ref, q_ref,
              *, n_heads):
    h = _rms(x_ref[...], g_ref[...]).astype(BF16)
    ql = _rms(jnp.dot(h, wdq_ref[...], preferred_element_type=F32), lg_ref[...]).astype(BF16)
    z = jnp.dot(ql, wq_ref[...], preferred_element_type=F32)
    cos, sin = cos_ref[...], sin_ref[...]
    n_pairs = n_heads // 2
    for pp in range(n_pairs):
        qn = [_rms(z[:, hh * QK_NOPE:(hh + 1) * QK_NOPE], ng_ref[...]) for hh in (2 * pp, 2 * pp + 1)]
        o1 = n_heads * QK_NOPE + pp * LANES
        o2 = o1 + n_pairs * LANES
        qr = _rope_rows(z[:, o1:o1 + LANES], z[:, o2:o2 + LANES], rg_ref[...], rgs_ref[...], cos, sin,
                        same_halves=False)
        q_ref[pp] = jnp.concatenate(qn + [qr], axis=-1).astype(BF16)


def _q_proj(x, g, w_dq, l_g, w_q, n_g, r_g, r_gs, cos, sin, n_heads, scale):
    b, s, d = x.shape
    tm = _row_tile(s, STREAM_ROWS)
    q_lora = w_dq.shape[1]
    n_pairs = n_heads // 2
    dq = 2 * QK_NOPE + LANES
    tab = pl.BlockSpec((tm, LANES), lambda bi, j: (j, 0))
    return pl.pallas_call(
        functools.partial(_q_kernel, n_heads=n_heads),
        out_shape=jax.ShapeDtypeStruct((b, n_pairs, s, dq), BF16),
        grid=(b, s // tm),
        in_specs=[pl.BlockSpec((None, tm, d), lambda bi, j: (bi, j, 0)), _resident((1, d)),
                  _resident(w_dq), _resident((1, q_lora)), _resident(w_q),
                  _resident((1, QK_NOPE)), _resident((1, LANES)), _resident((1, LANES)), tab, tab],
        out_specs=pl.BlockSpec((None, n_pairs, tm, dq), lambda bi, j: (bi, 0, j, 0)),
        compiler_params=_params("parallel", "parallel"),
        name="mla_q",
    )(x, g, _arr(w_dq), l_g, _arr(w_q), n_g * scale, r_g * scale, r_gs * scale, cos, sin)


def _attn_kernel(q_ref, k_ref, v_ref, o_ref, vx_ref, *, hg, tq, tk, nq, start, past):
    @pl.when(pl.program_id(2) == 0)
    def _():
        vx_ref[:, :, :V_HEAD] = v_ref[...]
        vx_ref[:, :, V_HEAD:] = jnp.ones(v_ref.shape, BF16)

    def attend(ti):
        nk = (_full_blocks(ti, tq, tk, start, past) + 1) * tk
        qpos = start + ti * tq + lax.broadcasted_iota(jnp.int32, (tq, 1), 0)
        kpos = start - past + (nk - tk) + lax.broadcasted_iota(jnp.int32, (1, tk), 1)
        visible = jnp.right_shift(kpos, CHUNK_SHIFT) <= jnp.right_shift(qpos, CHUNK_SHIFT)
        scores, probs, outs = [], [], []
        for hh in range(hg):
            lo = (hh % 2) * QK_NOPE
            q = jnp.concatenate([q_ref[hh // 2, :, lo:lo + QK_NOPE], q_ref[hh // 2, :, 2 * QK_NOPE:]], axis=-1)
            scores.append(lax.dot_general(q, k_ref[hh, :nk, :], (((1,), (1,)), ((), ())),
                                          preferred_element_type=F32))
        for s in scores:
            last = jnp.where(visible, s[:, nk - tk:], -jnp.inf)
            m = jnp.max(last, axis=-1, keepdims=True)
            if nk > tk:
                m = jnp.maximum(m, jnp.max(s[:, :nk - tk], axis=-1, keepdims=True))
                p = jnp.concatenate([jnp.exp2(s[:, :nk - tk] - m), jnp.exp2(last - m)], axis=-1)
            else:
                p = jnp.exp2(last - m)
            probs.append(p.astype(BF16))
        for hh, p in enumerate(probs):
            pv = jnp.dot(p, vx_ref[hh, :nk, :], preferred_element_type=F32)
            outs.append(pv[:, :V_HEAD] / pv[:, V_HEAD:])
        o_ref[...] = jnp.concatenate(outs, axis=-1).astype(BF16)

    if nq == 1:
        attend(0)
    else:
        i = pl.program_id(2)
        for ti in range(nq):
            pl.when(i == ti)(functools.partial(attend, ti))


def _full_blocks(i, tq, tk, start, past):
    first_chunk_start = ((start + i * tq) >> CHUNK_SHIFT) << CHUNK_SHIFT
    return (first_chunk_start - (start - past)) // tk


def _check_block_plan(nq, tq, tk, n_keys, start, past):
    for i in range(nq):
        n = _full_blocks(i, tq, tk, start, past)
        assert 0 <= n and (n + 1) * tk <= n_keys, (i, n)
        qc = (start + i * tq + np.arange(tq)) // CHUNK
        kc = (start - past + np.arange(n_keys)) // CHUNK
        vis = kc[None, :] <= qc[:, None]
        assert vis[:, :n * tk].all() and not vis[:, (n + 1) * tk:].any(), (i, n)
        assert vis[:, n * tk:(n + 1) * tk].any(axis=1).all()


def _attention(q, k, v, start, past):
    b, n_pairs, s, dq = q.shape
    n_heads, n_keys, dk = k.shape[1:]
    assert n_heads == 2 * n_pairs
    tq = _row_tile(s)
    tk = tq if n_keys % tq == 0 and s > tq else n_keys
    nq = s // tq
    hg = 4 if n_heads % 4 == 0 else 2
    _check_block_plan(nq, tq, tk, n_keys, start, past)
    return pl.pallas_call(
        functools.partial(_attn_kernel, hg=hg, tq=tq, tk=tk, nq=nq, start=start, past=past),
        out_shape=jax.ShapeDtypeStruct((b, s, n_heads * V_HEAD), BF16),
        grid=(b, n_heads // hg, nq),
        in_specs=[pl.BlockSpec((None, hg // 2, tq, dq), lambda bi, gi, i: (bi, gi, i, 0)),
                  pl.BlockSpec((None, hg, n_keys, dk), lambda bi, gi, i: (bi, gi, 0, 0)),
                  pl.BlockSpec((None, hg, n_keys, V_HEAD), lambda bi, gi, i: (bi, gi, 0, 0))],
        out_specs=pl.BlockSpec((None, tq, hg * V_HEAD), lambda bi, gi, i: (bi, i, gi)),
        scratch_shapes=[pltpu.VMEM((hg, n_keys, 2 * V_HEAD), BF16)],
        compiler_params=_params("parallel", "parallel", "arbitrary"),
        name="mla_attn",
    )(q, k, v)


def _twice(a):
    return jnp.concatenate([a, a], axis=-1)


_ROPE_SWAP = np.concatenate([np.arange(QK_ROPE // 2, QK_ROPE), np.arange(QK_ROPE // 2)])


def _rope_tables(start, s):
    inv = ROPE_BASE ** (-(jnp.arange(0, QK_ROPE, 2, dtype=F32) / QK_ROPE))
    ang = (start + jnp.arange(s, dtype=jnp.int32)).astype(F32)[:, None] * inv[None, :]
    cos, sin = jnp.cos(ang), jnp.sin(ang)
    return _twice(jnp.concatenate([cos, cos], axis=-1)), _twice(jnp.concatenate([-sin, sin], axis=-1))


def _prep_weights(p):
    n_heads = p["w_uk"].shape[1] // QK_NOPE
    assert n_heads % 2 == 0
    w = {"n_heads": n_heads}
    row = lambda a: a.reshape(1, -1).astype(F32)
    for name in ("ffn1_w_in", "ffn1_w_out", "ffn2_w_in", "ffn2_w_out", "pool_w", "w_dq", "w_o"):
        w[name] = p[name].astype(BF16)
    w["w_lat"] = jnp.concatenate(
        [p["w_dkv"], _twice(p["w_kr"]), _twice(p["w_kr"][:, _ROPE_SWAP])], axis=1).astype(BF16)
    w["kr_g"] = row(_twice(p["kr_norm"]))
    w["kr_gs"] = row(_twice(p["kr_norm"][_ROPE_SWAP]))
    w["w_kv"] = jnp.concatenate([p["w_uk"], p["w_uv"]], axis=1).astype(BF16)
    n_b, q_lora, _ = p["w_uq"].shape
    wq = p["w_uq"].reshape(n_b, q_lora, n_heads, QK_NOPE + QK_ROPE)
    nope, rope = wq[..., :QK_NOPE], wq[..., QK_NOPE:]
    flat = lambda a: a.reshape(n_b, q_lora, -1)
    w["w_q"] = jnp.concatenate([flat(nope), flat(rope), flat(rope[..., _ROPE_SWAP])], axis=-1).astype(BF16)
    w["qr_g"] = _twice(p["qr_norm"])
    w["qr_gs"] = _twice(p["qr_norm"][:, _ROPE_SWAP])
    return w


def _trunk(x, pool_prev, ckv_past, kr_past, start, p, w):
    b, s, d = x.shape
    past = 0 if ckv_past is None else ckv_past.shape[1]
    depth = p["ffn1_norm"].shape[0]
    n_a = p["pool_w"].shape[0]
    n_heads = w["n_heads"]
    row = lambda a: a.reshape(1, -1).astype(F32)
    cos, sin = _rope_tables(start, s)
    scale = float((QK_NOPE + QK_ROPE) ** -0.5 * np.log2(np.e))
    xf = x.reshape(b * s, d)
    new_pool = []
    for layer in range(depth):
        if layer == n_a:
            c_new, kr_new = _latent(xf, s, row(p["kv_norm"]), w["w_lat"], row(p["c_norm"]), w["kr_g"], w["kr_gs"],
                                    cos, sin)
            c_new = c_new.reshape(b, s, -1)
            kr_new = kr_new.reshape(b, s, LANES)
            c_all, kr_all = c_new, kr_new
            if past:
                c_all = jnp.concatenate([ckv_past.astype(F32), c_new], axis=1)
                kr_all = jnp.concatenate([_twice(kr_past.astype(F32)), kr_new], axis=1)
            keys, vals = _kv_expand(c_all, kr_all, w["w_kv"], row(p["kn_norm"]), n_heads)
        at = lambda name, idx=layer: _Stacked(w[name], idx)
        xf = _ffn(xf, row(p["ffn1_norm"][layer]), at("ffn1_w_in"), at("ffn1_w_out"))
        ffn2 = (row(p["ffn2_norm"][layer]), at("ffn2_w_in"), at("ffn2_w_out"))
        if layer < n_a:
            prev = jnp.zeros((b, POOL_HIST, d), F32) if pool_prev is None else \
                jnp.pad(pool_prev[layer].astype(F32), ((0, 0), (POOL_HIST - pool_prev.shape[2], 0), (0, 0)))
            fuse = s % FFN_ROWS == 0
            xm, st = _pool_mix(xf.reshape(b, s, d), prev, row(p["mix_norm"][layer]), at("pool_w"),
                               row(p["pool_scale"][layer]), start, ffn=ffn2 if fuse else None)
            new_pool.append(st[:, 1:])
            xf = xm.reshape(b * s, d) if fuse else _ffn(xm.reshape(b * s, d), *ffn2)
        else:
            i = layer - n_a
            q = _q_proj(xf.reshape(b, s, d), row(p["mix_norm"][layer]), at("w_dq", i), row(p["q_lat_norm"][i]),
                        at("w_q", i), row(p["qn_norm"][i]), row(w["qr_g"][i]), row(w["qr_gs"][i]), cos, sin,
                        n_heads, scale)
            att = _attention(q, keys, vals, start, past)
            xf = _ffn(xf, *ffn2, attn=att.reshape(b * s, -1), w_o=at("w_o", i))
    return xf.reshape(b, s, d), jnp.stack(new_pool, axis=0), c_new, kr_new[..., :QK_ROPE]


def kernel(x_prompt, x_sample, state_pool, cache_ckv, cache_krope, ffn1_norm, ffn1_w_in, ffn1_w_out, mix_norm, ffn2_norm, ffn2_w_in, ffn2_w_out, pool_w, pool_scale, kv_norm, w_dkv, c_norm, w_kr, kr_norm, w_uk, kn_norm, w_uv, w_dq, q_lat_norm, w_uq, qn_norm, qr_norm, w_o):
    p = dict(ffn1_norm=ffn1_norm, ffn1_w_in=ffn1_w_in, ffn1_w_out=ffn1_w_out, mix_norm=mix_norm,
             ffn2_norm=ffn2_norm, ffn2_w_in=ffn2_w_in, ffn2_w_out=ffn2_w_out, pool_w=pool_w,
             pool_scale=pool_scale, kv_norm=kv_norm, w_dkv=w_dkv, c_norm=c_norm, w_kr=w_kr, kr_norm=kr_norm,
             w_uk=w_uk, kn_norm=kn_norm, w_uv=w_uv, w_dq=w_dq, q_lat_norm=q_lat_norm, w_uq=w_uq,
             qn_norm=qn_norm, qr_norm=qr_norm, w_o=w_o)
    w = _prep_weights(p)
    y_p, pool_p, ckv_p, kr_p = _trunk(x_prompt, None, None, None, 0, p, w)
    y_s, pool_s, ckv_s, kr_s = _trunk(x_sample, state_pool, cache_ckv, cache_krope, cache_ckv.shape[1], p, w)
    return (y_p, y_s, pool_p, pool_s, ckv_p, kr_p, ckv_s, kr_s)
```

```python
import functools
from typing import NamedTuple

import numpy as np
import jax
import jax.numpy as jnp
from jax import lax
from jax.experimental import pallas as pl
from jax.experimental.pallas import tpu as pltpu

F32 = jnp.float32
BF16 = jnp.bfloat16

EPS = 1e-6
CHUNK = 64
CHUNK_SHIFT = 6
ROPE_BASE = 10000.0
POOL_WINDOWS = (2, 4, 8, 16)
POOL_HIST = 16
QK_NOPE = 128
QK_ROPE = 64
V_HEAD = 128
LANES = 128
VMEM_LIMIT_BYTES = 56 * 1024 * 1024
MXU_COLS = 256
FFN_ROWS = 1024
FFN_CHUNKS = 4
ATTN_ROWS = 512
STREAM_ROWS = 1024


def _rms(x, g, n=None):
    n = x.shape[-1] if n is None else n
    ms = jnp.sum(x * x, axis=-1, keepdims=True) * (1.0 / n)
    return x * lax.rsqrt(ms + EPS) * g


class _Stacked(NamedTuple):
    stack: jax.Array
    idx: int

    @property
    def shape(self):
        return self.stack.shape[1:]


def _arr(a):
    return a.stack if isinstance(a, _Stacked) else a


def _resident(a):
    shape = a if isinstance(a, tuple) and not isinstance(a, _Stacked) else tuple(a.shape)
    zeros = (0,) * len(shape)
    if isinstance(a, _Stacked):
        return pl.BlockSpec((None,) + shape, lambda *_: (a.idx,) + zeros, pipeline_mode=pl.Buffered(1))
    return pl.BlockSpec(shape, lambda *_: zeros, pipeline_mode=pl.Buffered(1))


def _params(*sem):
    return pltpu.CompilerParams(dimension_semantics=sem, vmem_limit_bytes=VMEM_LIMIT_BYTES)


def _row_tile(t, cap=FFN_ROWS):
    tm = cap
    while t % tm:
        tm //= 2
    assert tm >= 8, (t, cap)
    return tm


def _ffn_chunks(d_ff):
    units = d_ff // MXU_COLS
    assert units * MXU_COLS == d_ff
    n = min(FFN_CHUNKS, units)
    sizes = [(units + n - 1 - i) // n * MXU_COLS for i in range(n)]
    return [(sum(sizes[:i]), sizes[i]) for i in range(n)]


def _chunked_w_in(w_in):
    d_ff = w_in.shape[-1] // 2
    return jnp.concatenate([w_in[..., o + c0:o + c0 + wc] for c0, wc in _ffn_chunks(d_ff) for o in (0, d_ff)],
                           axis=-1)


def _ffn_body(x, g_ref, win_ref, wout_ref, d_ff):
    h = _rms(x, g_ref[...]).astype(BF16)
    y = None
    for c0, wc in _ffn_chunks(d_ff):
        gu = jnp.dot(h, win_ref[:, 2 * c0:2 * (c0 + wc)], preferred_element_type=F32)
        a = (jax.nn.silu(gu[:, :wc]) * gu[:, wc:]).astype(BF16)
        yc = jnp.dot(a, wout_ref[c0:c0 + wc, :], preferred_element_type=F32)
        y = yc if y is None else y + yc
    return x + 0.5 * y


def _ffn_kernel(x_ref, g_ref, win_ref, wout_ref, o_ref, *, d_ff):
    o_ref[...] = _ffn_body(x_ref[...], g_ref, win_ref, wout_ref, d_ff)


def _ffn_wo_kernel(x_ref, a_ref, wo_ref, g_ref, win_ref, wout_ref, o_ref, *, d_ff):
    x = x_ref[...] + jnp.dot(a_ref[...], wo_ref[...], preferred_element_type=F32)
    o_ref[...] = _ffn_body(x, g_ref, win_ref, wout_ref, d_ff)


def _ffn(x, g, w_in, w_out, attn=None, w_o=None):
    t, d = x.shape
    d_ff = w_out.shape[0]
    tm = _row_tile(t)
    row = pl.BlockSpec((tm, d), lambda i: (i, 0))
    w_specs = [_resident((1, d)), _resident(w_in), _resident(w_out)]
    if attn is None:
        kern, ins, specs = _ffn_kernel, (x, g, w_in, w_out), [row] + w_specs
    else:
        kern = _ffn_wo_kernel
        ins = (x, attn, w_o, g, w_in, w_out)
        specs = [row, pl.BlockSpec((tm, attn.shape[1]), lambda i: (i, 0)), _resident(w_o)] + w_specs
    return pl.pallas_call(
        functools.partial(kern, d_ff=d_ff),
        out_shape=jax.ShapeDtypeStruct((t, d), F32),
        grid=(t // tm,),
        in_specs=specs,
        out_specs=row,
        compiler_params=_params("parallel"),
        name="ffn" if attn is None else "ffn_wo",
    )(*map(_arr, ins))


def _pool_reset(j, prev_ref, carry_ref, start):
    @pl.when(j == 0)
    def _():
        prev = prev_ref[...]
        if start < POOL_HIST:
            rowpos = start - POOL_HIST + lax.broadcasted_iota(jnp.int32, (POOL_HIST, 1), 0)
            prev = jnp.where(rowpos >= 0, prev, 0.0)
        carry_ref[...] = prev


def _pool_deltas(j, x_ref, g_ref, st_ref, carry_ref, start, ts, group):
    h = _rms(x_ref[...], g_ref[...])
    ext = jnp.concatenate([carry_ref[...], h], axis=0)
    carry_ref[...] = h[ts - POOL_HIST:, :]
    st_ref[...] = h[ts - POOL_HIST:, :]
    pos = start + j * ts + lax.broadcasted_iota(jnp.int32, (ts, 1), 0)
    outs = []
    for gi, w in enumerate(POOL_WINDOWS):
        lo, hi = gi * group, (gi + 1) * group
        p = ext[:, lo:hi]
        k = 1
        while k < w:
            p = p + pltpu.roll(p, k, axis=0)
            k *= 2
        inv_cnt = 1.0 / jnp.minimum(pos + 1, w).astype(F32)
        outs.append((p[POOL_HIST:, :] * inv_cnt - h[:, lo:hi]).astype(BF16))
    return jnp.concatenate(outs, axis=-1)


def _pool_apply(x, dlt, w_ref, sc_ref):
    group = w_ref.shape[1]
    outs = [jnp.dot(dlt[:, gi * group:(gi + 1) * group], w_ref[gi], preferred_element_type=F32)
            for gi in range(len(POOL_WINDOWS))]
    return x + jnp.concatenate(outs, axis=-1) * sc_ref[...]


def _pool_kernel(x_ref, prev_ref, g_ref, w_ref, sc_ref, o_ref, st_ref, carry_ref, *, start, ts):
    j = pl.program_id(1)
    _pool_reset(j, prev_ref, carry_ref, start)
    dlt = _pool_deltas(j, x_ref, g_ref, st_ref, carry_ref, start, ts, w_ref.shape[1])
    o_ref[...] = _pool_apply(x_ref[...], dlt, w_ref, sc_ref)


def _pool_ffn_kernel(x_ref, prev_ref, g_ref, w_ref, sc_ref, fg_ref, win_ref, wout_ref, o_ref, st_ref, carry_ref,
                     *, start, ts, d_ff):
    j = pl.program_id(1)
    _pool_reset(j, prev_ref, carry_ref, start)
    dlt = _pool_deltas(j, x_ref, g_ref, st_ref, carry_ref, start, ts, w_ref.shape[1])
    xm = _pool_apply(x_ref[...], dlt, w_ref, sc_ref)
    o_ref[...] = _ffn_body(xm, fg_ref, win_ref, wout_ref, d_ff)


def _pool_mix(x, prev16, g, w_pool, scale, start, ffn=None):
    b, s, d = x.shape
    ts = _row_tile(s)
    assert ts >= POOL_HIST and len(POOL_WINDOWS) * w_pool.shape[1] == d
    seq = pl.BlockSpec((None, ts, d), lambda bi, j: (bi, j, 0))
    hist = pl.BlockSpec((None, POOL_HIST, d), lambda bi, j: (bi, 0, 0))
    ins = [x, prev16, g, w_pool, scale]
    specs = [seq, hist, _resident((1, d)), _resident(w_pool), _resident((1, d))]
    if ffn is None:
        kern = functools.partial(_pool_kernel, start=start, ts=ts)
    else:
        kern = functools.partial(_pool_ffn_kernel, start=start, ts=ts, d_ff=ffn[2].shape[0])
        ins += list(ffn)
        specs += [_resident((1, d)), _resident(ffn[1]), _resident(ffn[2])]
    return pl.pallas_call(
        kern,
        out_shape=(jax.ShapeDtypeStruct((b, s, d), F32), jax.ShapeDtypeStruct((b, POOL_HIST, d), F32)),
        grid=(b, s // ts),
        in_specs=specs,
        out_specs=(seq, hist),
        scratch_shapes=[pltpu.VMEM((POOL_HIST, d), F32)],
        compiler_params=_params("parallel", "arbitrary"),
        name="pool_mix" if ffn is None else "pool_ffn",
    )(*map(_arr, ins))


def _rope_rows(z, z_sw, g, g_sw, cos, sin, same_halves):
    zz = z * z
    if same_halves:
        r = lax.rsqrt(jnp.sum(zz, axis=-1, keepdims=True) * (1.0 / LANES) + EPS)
    else:
        lo = lax.broadcasted_iota(jnp.int32, (1, LANES), 1) < QK_ROPE
        r_lo = lax.rsqrt(jnp.sum(jnp.where(lo, zz, 0.0), axis=-1, keepdims=True) * (1.0 / QK_ROPE) + EPS)
        r_hi = lax.rsqrt(jnp.sum(jnp.where(lo, 0.0, zz), axis=-1, keepdims=True) * (1.0 / QK_ROPE) + EPS)
        r = jnp.where(lo, r_lo, r_hi)
    return (z * r) * g * cos + (z_sw * r) * g_sw * sin


def _latent_kernel(x_ref, g_ref, w_ref, cg_ref, rg_ref, rgs_ref, cos_ref, sin_ref, c_ref, kr_ref, *, kv_lora):
    h = _rms(x_ref[...], g_ref[...]).astype(BF16)
    z = jnp.dot(h, w_ref[...], preferred_element_type=F32)
    c_ref[...] = _rms(z[:, :kv_lora], cg_ref[...])
    kr_ref[...] = _rope_rows(z[:, kv_lora:kv_lora + LANES], z[:, kv_lora + LANES:],
                             rg_ref[...], rgs_ref[...], cos_ref[...], sin_ref[...], same_halves=True)


def _table_spec(tm, s):
    if tm <= s:
        assert s % tm == 0
        per = s // tm
        return pl.BlockSpec((tm, LANES), lambda i: (i % per, 0))
    return pl.BlockSpec((tm, LANES), lambda i: (0, 0))


def _tables_for(cos, sin, tm, s):
    if tm <= s:
        return cos, sin
    assert tm % s == 0
    return jnp.tile(cos, (tm // s, 1)), jnp.tile(sin, (tm // s, 1))


def _latent(x, s, g, w_lat, c_g, r_g, r_gs, cos, sin):
    t, d = x.shape
    kv_lora = c_g.shape[1]
    tm = _row_tile(t, STREAM_ROWS)
    cos_t, sin_t = _tables_for(cos, sin, tm, s)
    return pl.pallas_call(
        functools.partial(_latent_kernel, kv_lora=kv_lora),
        out_shape=(jax.ShapeDtypeStruct((t, kv_lora), F32), jax.ShapeDtypeStruct((t, LANES), F32)),
        grid=(t // tm,),
        in_specs=[pl.BlockSpec((tm, d), lambda i: (i, 0)), _resident((1, d)), _resident(w_lat.shape),
                  _resident((1, kv_lora)), _resident((1, LANES)), _resident((1, LANES)),
                  _table_spec(tm, s), _table_spec(tm, s)],
        out_specs=(pl.BlockSpec((tm, kv_lora), lambda i: (i, 0)), pl.BlockSpec((tm, LANES), lambda i: (i, 0))),
        compiler_params=_params("parallel"),
        name="mla_latent",
    )(x, g, w_lat, c_g, r_g, r_gs, cos_t, sin_t)


def _kv_kernel(c_ref, kr_ref, w_ref, g_ref, k_ref, v_ref, *, n_heads):
    z = jnp.dot(c_ref[...].astype(BF16), w_ref[...], preferred_element_type=F32)
    lo = lax.broadcasted_iota(jnp.int32, (1, LANES), 1) < QK_ROPE
    kr = kr_ref[...]
    kr_half = (jnp.where(lo, kr, 0.0).astype(BF16), jnp.where(lo, 0.0, kr).astype(BF16))
    for hh in range(n_heads):
        kn = _rms(z[:, hh * QK_NOPE:(hh + 1) * QK_NOPE], g_ref[...]).astype(BF16)
        k_ref[hh] = jnp.concatenate([kn, kr_half[hh % 2]], axis=-1)
        off = n_heads * QK_NOPE + hh * V_HEAD
        v_ref[hh] = z[:, off:off + V_HEAD].astype(BF16)


def _kv_expand(c_all, kr_all, w_kv, kn_g, n_heads):
    b, tk, kv_lora = c_all.shape
    tr = tk if tk % STREAM_ROWS else STREAM_ROWS
    return pl.pallas_call(
        functools.partial(_kv_kernel, n_heads=n_heads),
        out_shape=(jax.ShapeDtypeStruct((b, n_heads, tk, QK_NOPE + LANES), BF16),
                   jax.ShapeDtypeStruct((b, n_heads, tk, V_HEAD), BF16)),
        grid=(b, tk // tr),
        in_specs=[pl.BlockSpec((None, tr, kv_lora), lambda bi, j: (bi, j, 0)),
                  pl.BlockSpec((None, tr, LANES), lambda bi, j: (bi, j, 0)),
                  _resident(w_kv.shape), _resident((1, QK_NOPE))],
        out_specs=(pl.BlockSpec((None, n_heads, tr, QK_NOPE + LANES), lambda bi, j: (bi, 0, j, 0)),
                   pl.BlockSpec((None, n_heads, tr, V_HEAD), lambda bi, j: (bi, 0, j, 0))),
        compiler_params=_params("parallel", "parallel"),
        name="mla_kv",
    )(c_all, kr_all, w_kv, kn_g)


def _q_kernel(x_ref, g_ref, wdq_ref, lg_ref, wq_ref, ng_ref, rg_ref, rgs_ref, cos_ref, sin_ref, q_ref,
              *, n_heads):
    h = _rms(x_ref[...], g_ref[...]).astype(BF16)
    ql = _rms(jnp.dot(h, wdq_ref[...], preferred_element_type=F32), lg_ref[...]).astype(BF16)
    z = jnp.dot(ql, wq_ref[...], preferred_element_type=F32)
    cos, sin = cos_ref[...], sin_ref[...]
    n_pairs = n_heads // 2
    for pp in range(n_pairs):
        qn = [_rms(z[:, hh * QK_NOPE:(hh + 1) * QK_NOPE], ng_ref[...]) for hh in (2 * pp, 2 * pp + 1)]
        o1 = n_heads * QK_NOPE + pp * LANES
        o2 = o1 + n_pairs * LANES
        qr = _rope_rows(z[:, o1:o1 + LANES], z[:, o2:o2 + LANES], rg_ref[...], rgs_ref[...], cos, sin,
                        same_halves=False)
        q_ref[pp] = jnp.concatenate(qn + [qr], axis=-1).astype(BF16)


def _q_proj(x, g, w_dq, l_g, w_q, n_g, r_g, r_gs, cos, sin, n_heads, scale):
    b, s, d = x.shape
    tm = _row_tile(s, STREAM_ROWS)
    q_lora = w_dq.shape[1]
    n_pairs = n_heads // 2
    dq = 2 * QK_NOPE + LANES
    tab = pl.BlockSpec((tm, LANES), lambda bi, j: (j, 0))
    return pl.pallas_call(
        functools.partial(_q_kernel, n_heads=n_heads),
        out_shape=jax.ShapeDtypeStruct((b, n_pairs, s, dq), BF16),
        grid=(b, s // tm),
        in_specs=[pl.BlockSpec((None, tm, d), lambda bi, j: (bi, j, 0)), _resident((1, d)),
                  _resident(w_dq), _resident((1, q_lora)), _resident(w_q),
                  _resident((1, QK_NOPE)), _resident((1, LANES)), _resident((1, LANES)), tab, tab],
        out_specs=pl.BlockSpec((None, n_pairs, tm, dq), lambda bi, j: (bi, 0, j, 0)),
        compiler_params=_params("parallel", "parallel"),
        name="mla_q",
    )(x, g, _arr(w_dq), l_g, _arr(w_q), n_g * scale, r_g * scale, r_gs * scale, cos, sin)


def _attn_kernel(q_ref, k_ref, v_ref, o_ref, vx_ref, *, hg, tq, tk, nq, start, past):
    @pl.when(pl.program_id(2) == 0)
    def _():
        vx_ref[:, :, :V_HEAD] = v_ref[...]
        vx_ref[:, :, V_HEAD:] = jnp.ones(v_ref.shape, BF16)

    def attend(ti):
        nk = (_full_blocks(ti, tq, tk, start, past) + 1) * tk
        qpos = start + ti * tq + lax.broadcasted_iota(jnp.int32, (tq, 1), 0)
        kpos = start - past + (nk - tk) + lax.broadcasted_iota(jnp.int32, (1, tk), 1)
        visible = jnp.right_shift(kpos, CHUNK_SHIFT) <= jnp.right_shift(qpos, CHUNK_SHIFT)
        scores, probs, outs = [], [], []
        for hh in range(hg):
            lo = (hh % 2) * QK_NOPE
            q = jnp.concatenate([q_ref[hh // 2, :, lo:lo + QK_NOPE], q_ref[hh // 2, :, 2 * QK_NOPE:]], axis=-1)
            scores.append(lax.dot_general(q, k_ref[hh, :nk, :], (((1,), (1,)), ((), ())),
                                          preferred_element_type=F32))
        for s in scores:
            last = jnp.where(visible, s[:, nk - tk:], -jnp.inf)
            m = jnp.max(last, axis=-1, keepdims=True)
            if nk > tk:
                m = jnp.maximum(m, jnp.max(s[:, :nk - tk], axis=-1, keepdims=True))
                p = jnp.concatenate([jnp.exp2(s[:, :nk - tk] - m), jnp.exp2(last - m)], axis=-1)
            else:
                p = jnp.exp2(last - m)
            probs.append(p.astype(BF16))
        for hh, p in enumerate(probs):
            pv = jnp.dot(p, vx_ref[hh, :nk, :], preferred_element_type=F32)
            outs.append(pv[:, :V_HEAD] / pv[:, V_HEAD:])
        o_ref[...] = jnp.concatenate(outs, axis=-1).astype(BF16)

    if nq == 1:
        attend(0)
    else:
        i = pl.program_id(2)
        for ti in range(nq):
            pl.when(i == ti)(functools.partial(attend, ti))


def _full_blocks(i, tq, tk, start, past):
    first_chunk_start = ((start + i * tq) >> CHUNK_SHIFT) << CHUNK_SHIFT
    return (first_chunk_start - (start - past)) // tk


def _check_block_plan(nq, tq, tk, n_keys, start, past):
    for i in range(nq):
        n = _full_blocks(i, tq, tk, start, past)
        assert 0 <= n and (n + 1) * tk <= n_keys, (i, n)
        qc = (start + i * tq + np.arange(tq)) // CHUNK
        kc = (start - past + np.arange(n_keys)) // CHUNK
        vis = kc[None, :] <= qc[:, None]
        assert vis[:, :n * tk].all() and not vis[:, (n + 1) * tk:].any(), (i, n)
        assert vis[:, n * tk:(n + 1) * tk].any(axis=1).all()


def _attention(q, k, v, start, past):
    b, n_pairs, s, dq = q.shape
    n_heads, n_keys, dk = k.shape[1:]
    assert n_heads == 2 * n_pairs
    tq = _row_tile(s, ATTN_ROWS)
    tk = tq if n_keys % tq == 0 and s > tq else n_keys
    nq = s // tq
    hg = 4 if n_heads % 4 == 0 else 2
    _check_block_plan(nq, tq, tk, n_keys, start, past)
    return pl.pallas_call(
        functools.partial(_attn_kernel, hg=hg, tq=tq, tk=tk, nq=nq, start=start, past=past),
        out_shape=jax.ShapeDtypeStruct((b, s, n_heads * V_HEAD), BF16),
        grid=(b, n_heads // hg, nq),
        in_specs=[pl.BlockSpec((None, hg // 2, tq, dq), lambda bi, gi, i: (bi, gi, i, 0)),
                  pl.BlockSpec((None, hg, n_keys, dk), lambda bi, gi, i: (bi, gi, 0, 0)),
                  pl.BlockSpec((None, hg, n_keys, V_HEAD), lambda bi, gi, i: (bi, gi, 0, 0))],
        out_specs=pl.BlockSpec((None, tq, hg * V_HEAD), lambda bi, gi, i: (bi, i, gi)),
        scratch_shapes=[pltpu.VMEM((hg, n_keys, 2 * V_HEAD), BF16)],
        compiler_params=_params("parallel", "parallel", "arbitrary"),
        name="mla_attn",
    )(q, k, v)


def _twice(a):
    return jnp.concatenate([a, a], axis=-1)


_ROPE_SWAP = np.concatenate([np.arange(QK_ROPE // 2, QK_ROPE), np.arange(QK_ROPE // 2)])


def _rope_tables(start, s):
    inv = ROPE_BASE ** (-(jnp.arange(0, QK_ROPE, 2, dtype=F32) / QK_ROPE))
    ang = (start + jnp.arange(s, dtype=jnp.int32)).astype(F32)[:, None] * inv[None, :]
    cos, sin = jnp.cos(ang), jnp.sin(ang)
    return _twice(jnp.concatenate([cos, cos], axis=-1)), _twice(jnp.concatenate([-sin, sin], axis=-1))


def _prep_weights(p):
    n_heads = p["w_uk"].shape[1] // QK_NOPE
    assert n_heads % 2 == 0
    w = {"n_heads": n_heads}
    row = lambda a: a.reshape(1, -1).astype(F32)
    for name in ("ffn1_w_out", "ffn2_w_out", "pool_w", "w_dq", "w_o"):
        w[name] = p[name].astype(BF16)
    for name in ("ffn1_w_in", "ffn2_w_in"):
        w[name] = _chunked_w_in(p[name]).astype(BF16)
    w["w_lat"] = jnp.concatenate(
        [p["w_dkv"], _twice(p["w_kr"]), _twice(p["w_kr"][:, _ROPE_SWAP])], axis=1).astype(BF16)
    w["kr_g"] = row(_twice(p["kr_norm"]))
    w["kr_gs"] = row(_twice(p["kr_norm"][_ROPE_SWAP]))
    w["w_kv"] = jnp.concatenate([p["w_uk"], p["w_uv"]], axis=1).astype(BF16)
    n_b, q_lora, _ = p["w_uq"].shape
    wq = p["w_uq"].reshape(n_b, q_lora, n_heads, QK_NOPE + QK_ROPE)
    nope, rope = wq[..., :QK_NOPE], wq[..., QK_NOPE:]
    flat = lambda a: a.reshape(n_b, q_lora, -1)
    w["w_q"] = jnp.concatenate([flat(nope), flat(rope), flat(rope[..., _ROPE_SWAP])], axis=-1).astype(BF16)
    w["qr_g"] = _twice(p["qr_norm"])
    w["qr_gs"] = _twice(p["qr_norm"][:, _ROPE_SWAP])
    return w


def _trunk(x, pool_prev, ckv_past, kr_past, start, p, w):
    b, s, d = x.shape
    past = 0 if ckv_past is None else ckv_past.shape[1]
    depth = p["ffn1_norm"].shape[0]
    n_a = p["pool_w"].shape[0]
    n_heads = w["n_heads"]
    row = lambda a: a.reshape(1, -1).astype(F32)
    cos, sin = _rope_tables(start, s)
    scale = float((QK_NOPE + QK_ROPE) ** -0.5 * np.log2(np.e))
    xf = x.reshape(b * s, d)
    new_pool = []
    for layer in range(depth):
        if layer == n_a:
            c_new, kr_new = _latent(xf, s, row(p["kv_norm"]), w["w_lat"], row(p["c_norm"]), w["kr_g"], w["kr_gs"],
                                    cos, sin)
            c_new = c_new.reshape(b, s, -1)
            kr_new = kr_new.reshape(b, s, LANES)
            c_all, kr_all = c_new, kr_new
            if past:
                c_all = jnp.concatenate([ckv_past.astype(F32), c_new], axis=1)
                kr_all = jnp.concatenate([_twice(kr_past.astype(F32)), kr_new], axis=1)
            keys, vals = _kv_expand(c_all, kr_all, w["w_kv"], row(p["kn_norm"]), n_heads)
        at = lambda name, idx=layer: _Stacked(w[name], idx)
        xf = _ffn(xf, row(p["ffn1_norm"][layer]), at("ffn1_w_in"), at("ffn1_w_out"))
        ffn2 = (row(p["ffn2_norm"][layer]), at("ffn2_w_in"), at("ffn2_w_out"))
        if layer < n_a:
            prev = jnp.zeros((b, POOL_HIST, d), F32) if pool_prev is None else \
                jnp.pad(pool_prev[layer].astype(F32), ((0, 0), (POOL_HIST - pool_prev.shape[2], 0), (0, 0)))
            fuse = s % FFN_ROWS == 0
            xm, st = _pool_mix(xf.reshape(b, s, d), prev, row(p["mix_norm"][layer]), at("pool_w"),
                               row(p["pool_scale"][layer]), start, ffn=ffn2 if fuse else None)
            new_pool.append(st[:, 1:])
            xf = xm.reshape(b * s, d) if fuse else _ffn(xm.reshape(b * s, d), *ffn2)
        else:
            i = layer - n_a
            q = _q_proj(xf.reshape(b, s, d), row(p["mix_norm"][layer]), at("w_dq", i), row(p["q_lat_norm"][i]),
                        at("w_q", i), row(p["qn_norm"][i]), row(w["qr_g"][i]), row(w["qr_gs"][i]), cos, sin,
                        n_heads, scale)
            att = _attention(q, keys, vals, start, past)
            xf = _ffn(xf, *ffn2, attn=att.reshape(b * s, -1), w_o=at("w_o", i))
    return xf.reshape(b, s, d), jnp.stack(new_pool, axis=0), c_new, kr_new[..., :QK_ROPE]


def kernel(x_prompt, x_sample, state_pool, cache_ckv, cache_krope, ffn1_norm, ffn1_w_in, ffn1_w_out, mix_norm, ffn2_norm, ffn2_w_in, ffn2_w_out, pool_w, pool_scale, kv_norm, w_dkv, c_norm, w_kr, kr_norm, w_uk, kn_norm, w_uv, w_dq, q_lat_norm, w_uq, qn_norm, qr_norm, w_o):
    p = dict(ffn1_norm=ffn1_norm, ffn1_w_in=ffn1_w_in, ffn1_w_out=ffn1_w_out, mix_norm=mix_norm,
             ffn2_norm=ffn2_norm, ffn2_w_in=ffn2_w_in, ffn2_w_out=ffn2_w_out, pool_w=pool_w,
             pool_scale=pool_scale, kv_norm=kv_norm, w_dkv=w_dkv, c_norm=c_norm, w_kr=w_kr, kr_norm=kr_norm,
             w_uk=w_uk, kn_norm=kn_norm, w_uv=w_uv, w_dq=w_dq, q_lat_norm=q_lat_norm, w_uq=w_uq,
             qn_norm=qn_norm, qr_norm=qr_norm, w_o=w_o)
    w = _prep_weights(p)
    y_p, pool_p, ckv_p, kr_p = _trunk(x_prompt, None, None, None, 0, p, w)
    y_s, pool_s, ckv_s, kr_s = _trunk(x_sample, state_pool, cache_ckv, cache_krope, cache_ckv.shape[1], p, w)
    return (y_p, y_s, pool_p, pool_s, ckv_p, kr_p, ckv_s, kr_s)
```

```python
import functools
from typing import NamedTuple

import numpy as np
import jax
import jax.numpy as jnp
from jax import lax
from jax.experimental import pallas as pl
from jax.experimental.pallas import tpu as pltpu

F32 = jnp.float32
BF16 = jnp.bfloat16

EPS = 1e-6
CHUNK = 64
CHUNK_SHIFT = 6
ROPE_BASE = 10000.0
POOL_WINDOWS = (2, 4, 8, 16)
POOL_HIST = 16
QK_NOPE = 128
QK_ROPE = 64
V_HEAD = 128
LANES = 128
VMEM_LIMIT_BYTES = 56 * 1024 * 1024
MXU_COLS = 256
FFN_ROWS = 1024
FFN_CHUNKS = 4
ATTN_ROWS = 512
ATTN_SUB_ROWS = 256
STREAM_ROWS = 1024


def _rms(x, g, n=None):
    n = x.shape[-1] if n is None else n
    ms = jnp.sum(x * x, axis=-1, keepdims=True) * (1.0 / n)
    return x * lax.rsqrt(ms + EPS) * g


class _Stacked(NamedTuple):
    stack: jax.Array
    idx: int

    @property
    def shape(self):
        return self.stack.shape[1:]


def _arr(a):
    return a.stack if isinstance(a, _Stacked) else a


def _resident(a):
    shape = a if isinstance(a, tuple) and not isinstance(a, _Stacked) else tuple(a.shape)
    zeros = (0,) * len(shape)
    if isinstance(a, _Stacked):
        return pl.BlockSpec((None,) + shape, lambda *_: (a.idx,) + zeros, pipeline_mode=pl.Buffered(1))
    return pl.BlockSpec(shape, lambda *_: zeros, pipeline_mode=pl.Buffered(1))


def _params(*sem):
    return pltpu.CompilerParams(dimension_semantics=sem, vmem_limit_bytes=VMEM_LIMIT_BYTES)


def _row_tile(t, cap=FFN_ROWS):
    tm = cap
    while t % tm:
        tm //= 2
    assert tm >= 8, (t, cap)
    return tm


def _ffn_chunks(d_ff):
    units = d_ff // MXU_COLS
    assert units * MXU_COLS == d_ff
    n = min(FFN_CHUNKS, units)
    sizes = [(units + n - 1 - i) // n * MXU_COLS for i in range(n)]
    return [(sum(sizes[:i]), sizes[i]) for i in range(n)]


def _ffn_body(x, g_ref, win_ref, wout_ref, d_ff):
    h = _rms(x, g_ref[...]).astype(BF16)
    y = None
    for c0, wc in _ffn_chunks(d_ff):
        gate = jnp.dot(h, win_ref[:, c0:c0 + wc], preferred_element_type=F32)
        up = jnp.dot(h, win_ref[:, d_ff + c0:d_ff + c0 + wc], preferred_element_type=F32)
        a = (jax.nn.silu(gate) * up).astype(BF16)
        yc = jnp.dot(a, wout_ref[c0:c0 + wc, :], preferred_element_type=F32)
        y = yc if y is None else y + yc
    return x + 0.5 * y


def _ffn_kernel(x_ref, g_ref, win_ref, wout_ref, o_ref, *, d_ff):
    o_ref[...] = _ffn_body(x_ref[...], g_ref, win_ref, wout_ref, d_ff)


def _ffn_wo_kernel(x_ref, a_ref, wo_ref, g_ref, win_ref, wout_ref, o_ref, *, d_ff):
    x = x_ref[...] + jnp.dot(a_ref[...], wo_ref[...], preferred_element_type=F32)
    o_ref[...] = _ffn_body(x, g_ref, win_ref, wout_ref, d_ff)


def _ffn(x, g, w_in, w_out, attn=None, w_o=None):
    t, d = x.shape
    d_ff = w_out.shape[0]
    tm = _row_tile(t)
    row = pl.BlockSpec((tm, d), lambda i: (i, 0))
    w_specs = [_resident((1, d)), _resident(w_in), _resident(w_out)]
    if attn is None:
        kern, ins, specs = _ffn_kernel, (x, g, w_in, w_out), [row] + w_specs
    else:
        kern = _ffn_wo_kernel
        ins = (x, attn, w_o, g, w_in, w_out)
        specs = [row, pl.BlockSpec((tm, attn.shape[1]), lambda i: (i, 0)), _resident(w_o)] + w_specs
    return pl.pallas_call(
        functools.partial(kern, d_ff=d_ff),
        out_shape=jax.ShapeDtypeStruct((t, d), F32),
        grid=(t // tm,),
        in_specs=specs,
        out_specs=row,
        compiler_params=_params("parallel"),
        name="ffn" if attn is None else "ffn_wo",
    )(*map(_arr, ins))


def _pool_reset(j, prev_ref, carry_ref, start):
    @pl.when(j == 0)
    def _():
        prev = prev_ref[...]
        if start < POOL_HIST:
            rowpos = start - POOL_HIST + lax.broadcasted_iota(jnp.int32, (POOL_HIST, 1), 0)
            prev = jnp.where(rowpos >= 0, prev, 0.0)
        carry_ref[...] = prev


def _pool_deltas(j, x_ref, g_ref, st_ref, carry_ref, start, ts, group):
    h = _rms(x_ref[...], g_ref[...])
    ext = jnp.concatenate([carry_ref[...], h], axis=0)
    carry_ref[...] = h[ts - POOL_HIST:, :]
    st_ref[...] = h[ts - POOL_HIST:, :]
    pos = start + j * ts + lax.broadcasted_iota(jnp.int32, (ts, 1), 0)
    outs = []
    for gi, w in enumerate(POOL_WINDOWS):
        lo, hi = gi * group, (gi + 1) * group
        p = ext[:, lo:hi]
        k = 1
        while k < w:
            p = p + pltpu.roll(p, k, axis=0)
            k *= 2
        inv_cnt = 1.0 / jnp.minimum(pos + 1, w).astype(F32)
        outs.append((p[POOL_HIST:, :] * inv_cnt - h[:, lo:hi]).astype(BF16))
    return jnp.concatenate(outs, axis=-1)


def _pool_apply(x, dlt, w_ref, sc_ref):
    group = w_ref.shape[1]
    outs = [jnp.dot(dlt[:, gi * group:(gi + 1) * group], w_ref[gi], preferred_element_type=F32)
            for gi in range(len(POOL_WINDOWS))]
    return x + jnp.concatenate(outs, axis=-1) * sc_ref[...]


def _pool_kernel(x_ref, prev_ref, g_ref, w_ref, sc_ref, o_ref, st_ref, carry_ref, *, start, ts):
    j = pl.program_id(1)
    _pool_reset(j, prev_ref, carry_ref, start)
    dlt = _pool_deltas(j, x_ref, g_ref, st_ref, carry_ref, start, ts, w_ref.shape[1])
    o_ref[...] = _pool_apply(x_ref[...], dlt, w_ref, sc_ref)


def _pool_ffn_kernel(x_ref, prev_ref, g_ref, w_ref, sc_ref, fg_ref, win_ref, wout_ref, o_ref, st_ref, carry_ref,
                     *, start, ts, d_ff):
    j = pl.program_id(1)
    _pool_reset(j, prev_ref, carry_ref, start)
    dlt = _pool_deltas(j, x_ref, g_ref, st_ref, carry_ref, start, ts, w_ref.shape[1])
    xm = _pool_apply(x_ref[...], dlt, w_ref, sc_ref)
    o_ref[...] = _ffn_body(xm, fg_ref, win_ref, wout_ref, d_ff)


def _pool_mix(x, prev16, g, w_pool, scale, start, ffn=None):
    b, s, d = x.shape
    ts = _row_tile(s)
    assert ts >= POOL_HIST and len(POOL_WINDOWS) * w_pool.shape[1] == d
    seq = pl.BlockSpec((None, ts, d), lambda bi, j: (bi, j, 0))
    hist = pl.BlockSpec((None, POOL_HIST, d), lambda bi, j: (bi, 0, 0))
    ins = [x, prev16, g, w_pool, scale]
    specs = [seq, hist, _resident((1, d)), _resident(w_pool), _resident((1, d))]
    if ffn is None:
        kern = functools.partial(_pool_kernel, start=start, ts=ts)
    else:
        kern = functools.partial(_pool_ffn_kernel, start=start, ts=ts, d_ff=ffn[2].shape[0])
        ins += list(ffn)
        specs += [_resident((1, d)), _resident(ffn[1]), _resident(ffn[2])]
    return pl.pallas_call(
        kern,
        out_shape=(jax.ShapeDtypeStruct((b, s, d), F32), jax.ShapeDtypeStruct((b, POOL_HIST, d), F32)),
        grid=(b, s // ts),
        in_specs=specs,
        out_specs=(seq, hist),
        scratch_shapes=[pltpu.VMEM((POOL_HIST, d), F32)],
        compiler_params=_params("parallel", "arbitrary"),
        name="pool_mix" if ffn is None else "pool_ffn",
    )(*map(_arr, ins))


def _rope_rows(z, z_sw, g, g_sw, cos, sin, same_halves):
    zz = z * z
    if same_halves:
        r = lax.rsqrt(jnp.sum(zz, axis=-1, keepdims=True) * (1.0 / LANES) + EPS)
    else:
        lo = lax.broadcasted_iota(jnp.int32, (1, LANES), 1) < QK_ROPE
        r_lo = lax.rsqrt(jnp.sum(jnp.where(lo, zz, 0.0), axis=-1, keepdims=True) * (1.0 / QK_ROPE) + EPS)
        r_hi = lax.rsqrt(jnp.sum(jnp.where(lo, 0.0, zz), axis=-1, keepdims=True) * (1.0 / QK_ROPE) + EPS)
        r = jnp.where(lo, r_lo, r_hi)
    return (z * r) * g * cos + (z_sw * r) * g_sw * sin


def _latent_kernel(x_ref, g_ref, w_ref, cg_ref, rg_ref, rgs_ref, cos_ref, sin_ref, c_ref, kr_ref, *, kv_lora):
    h = _rms(x_ref[...], g_ref[...]).astype(BF16)
    z = jnp.dot(h, w_ref[...], preferred_element_type=F32)
    c_ref[...] = _rms(z[:, :kv_lora], cg_ref[...])
    kr_ref[...] = _rope_rows(z[:, kv_lora:kv_lora + LANES], z[:, kv_lora + LANES:],
                             rg_ref[...], rgs_ref[...], cos_ref[...], sin_ref[...], same_halves=True)


def _table_spec(tm, s):
    if tm <= s:
        assert s % tm == 0
        per = s // tm
        return pl.BlockSpec((tm, LANES), lambda i: (i % per, 0))
    return pl.BlockSpec((tm, LANES), lambda i: (0, 0))


def _tables_for(cos, sin, tm, s):
    if tm <= s:
        return cos, sin
    assert tm % s == 0
    return jnp.tile(cos, (tm // s, 1)), jnp.tile(sin, (tm // s, 1))


def _latent(x, s, g, w_lat, c_g, r_g, r_gs, cos, sin):
    t, d = x.shape
    kv_lora = c_g.shape[1]
    tm = _row_tile(t, STREAM_ROWS)
    cos_t, sin_t = _tables_for(cos, sin, tm, s)
    return pl.pallas_call(
        functools.partial(_latent_kernel, kv_lora=kv_lora),
        out_shape=(jax.ShapeDtypeStruct((t, kv_lora), F32), jax.ShapeDtypeStruct((t, LANES), F32)),
        grid=(t // tm,),
        in_specs=[pl.BlockSpec((tm, d), lambda i: (i, 0)), _resident((1, d)), _resident(w_lat.shape),
                  _resident((1, kv_lora)), _resident((1, LANES)), _resident((1, LANES)),
                  _table_spec(tm, s), _table_spec(tm, s)],
        out_specs=(pl.BlockSpec((tm, kv_lora), lambda i: (i, 0)), pl.BlockSpec((tm, LANES), lambda i: (i, 0))),
        compiler_params=_params("parallel"),
        name="mla_latent",
    )(x, g, w_lat, c_g, r_g, r_gs, cos_t, sin_t)


def _kv_kernel(c_ref, kr_ref, w_ref, g_ref, k_ref, v_ref, *, n_heads):
    z = jnp.dot(c_ref[...].astype(BF16), w_ref[...], preferred_element_type=F32)
    lo = lax.broadcasted_iota(jnp.int32, (1, LANES), 1) < QK_ROPE
    kr = kr_ref[...]
    kr_half = (jnp.where(lo, kr, 0.0).astype(BF16), jnp.where(lo, 0.0, kr).astype(BF16))
    for hh in range(n_heads):
        kn = _rms(z[:, hh * QK_NOPE:(hh + 1) * QK_NOPE], g_ref[...]).astype(BF16)
        k_ref[hh] = jnp.concatenate([kn, kr_half[hh % 2]], axis=-1)
        off = n_heads * QK_NOPE + hh * V_HEAD
        v_ref[hh] = z[:, off:off + V_HEAD].astype(BF16)


def _kv_expand(c_all, kr_all, w_kv, kn_g, n_heads):
    b, tk, kv_lora = c_all.shape
    tr = tk if tk % STREAM_ROWS else STREAM_ROWS
    return pl.pallas_call(
        functools.partial(_kv_kernel, n_heads=n_heads),
        out_shape=(jax.ShapeDtypeStruct((b, n_heads, tk, QK_NOPE + LANES), BF16),
                   jax.ShapeDtypeStruct((b, n_heads, tk, V_HEAD), BF16)),
        grid=(b, tk // tr),
        in_specs=[pl.BlockSpec((None, tr, kv_lora), lambda bi, j: (bi, j, 0)),
                  pl.BlockSpec((None, tr, LANES), lambda bi, j: (bi, j, 0)),
                  _resident(w_kv.shape), _resident((1, QK_NOPE))],
        out_specs=(pl.BlockSpec((None, n_heads, tr, QK_NOPE + LANES), lambda bi, j: (bi, 0, j, 0)),
                   pl.BlockSpec((None, n_heads, tr, V_HEAD), lambda bi, j: (bi, 0, j, 0))),
        compiler_params=_params("parallel", "parallel"),
        name="mla_kv",
    )(c_all, kr_all, w_kv, kn_g)


def _q_kernel(x_ref, g_ref, wdq_ref, lg_ref, wq_ref, ng_ref, rg_ref, rgs_ref, cos_ref, sin_ref, q_ref,
              *, n_heads):
    h = _rms(x_ref[...], g_ref[...]).astype(BF16)
    ql = _rms(jnp.dot(h, wdq_ref[...], preferred_element_type=F32), lg_ref[...]).astype(BF16)
    z = jnp.dot(ql, wq_ref[...], preferred_element_type=F32)
    cos, sin = cos_ref[...], sin_ref[...]
    n_pairs = n_heads // 2
    for pp in range(n_pairs):
        qn = [_rms(z[:, hh * QK_NOPE:(hh + 1) * QK_NOPE], ng_ref[...]) for hh in (2 * pp, 2 * pp + 1)]
        o1 = n_heads * QK_NOPE + pp * LANES
        o2 = o1 + n_pairs * LANES
        qr = _rope_rows(z[:, o1:o1 + LANES], z[:, o2:o2 + LANES], rg_ref[...], rgs_ref[...], cos, sin,
                        same_halves=False)
        q_ref[pp] = jnp.concatenate(qn + [qr], axis=-1).astype(BF16)


def _q_proj(x, g, w_dq, l_g, w_q, n_g, r_g, r_gs, cos, sin, n_heads, scale):
    b, s, d = x.shape
    tm = _row_tile(s, STREAM_ROWS)
    q_lora = w_dq.shape[1]
    n_pairs = n_heads // 2
    dq = 2 * QK_NOPE + LANES
    tab = pl.BlockSpec((tm, LANES), lambda bi, j: (j, 0))
    return pl.pallas_call(
        functools.partial(_q_kernel, n_heads=n_heads),
        out_shape=jax.ShapeDtypeStruct((b, n_pairs, s, dq), BF16),
        grid=(b, s // tm),
        in_specs=[pl.BlockSpec((None, tm, d), lambda bi, j: (bi, j, 0)), _resident((1, d)),
                  _resident(w_dq), _resident((1, q_lora)), _resident(w_q),
                  _resident((1, QK_NOPE)), _resident((1, LANES)), _resident((1, LANES)), tab, tab],
        out_specs=pl.BlockSpec((None, n_pairs, tm, dq), lambda bi, j: (bi, 0, j, 0)),
        compiler_params=_params("parallel", "parallel"),
        name="mla_q",
    )(x, g, _arr(w_dq), l_g, _arr(w_q), n_g * scale, r_g * scale, r_gs * scale, cos, sin)


def _attn_kernel(q_ref, k_ref, v_ref, o_ref, vx_ref, *, hg, tq, ts, tk, nq, start, past):
    @pl.when(pl.program_id(2) == 0)
    def _():
        vx_ref[:, :, :V_HEAD] = v_ref[...]
        vx_ref[:, :, V_HEAD:] = jnp.ones(v_ref.shape, BF16)

    def attend(ti):
        chains = []
        for sub in range(tq // ts):
            si = ti * (tq // ts) + sub
            nk = (_full_blocks(si, ts, tk, start, past) + 1) * tk
            qpos = start + si * ts + lax.broadcasted_iota(jnp.int32, (ts, 1), 0)
            kpos = start - past + (nk - tk) + lax.broadcasted_iota(jnp.int32, (1, tk), 1)
            visible = jnp.right_shift(kpos, CHUNK_SHIFT) <= jnp.right_shift(qpos, CHUNK_SHIFT)
            chains += [(sub * ts, hh, nk, visible) for hh in range(hg)]
        scores, probs = [], []
        for r0, hh, nk, _ in chains:
            lo = (hh % 2) * QK_NOPE
            q = jnp.concatenate([q_ref[hh // 2, r0:r0 + ts, lo:lo + QK_NOPE],
                                 q_ref[hh // 2, r0:r0 + ts, 2 * QK_NOPE:]], axis=-1)
            scores.append(lax.dot_general(q, k_ref[hh, :nk, :], (((1,), (1,)), ((), ())),
                                          preferred_element_type=F32))
        for (_, _, nk, visible), s in zip(chains, scores):
            last = jnp.where(visible, s[:, nk - tk:], -jnp.inf)
            m = jnp.max(last, axis=-1, keepdims=True)
            if nk > tk:
                m = jnp.maximum(m, jnp.max(s[:, :nk - tk], axis=-1, keepdims=True))
                p = jnp.concatenate([jnp.exp2(s[:, :nk - tk] - m), jnp.exp2(last - m)], axis=-1)
            else:
                p = jnp.exp2(last - m)
            probs.append(p.astype(BF16))
        for (r0, hh, nk, _), p in zip(chains, probs):
            pv = jnp.dot(p, vx_ref[hh, :nk, :], preferred_element_type=F32)
            o_ref[r0:r0 + ts, hh * V_HEAD:(hh + 1) * V_HEAD] = (pv[:, :V_HEAD] / pv[:, V_HEAD:]).astype(BF16)

    if nq == 1:
        attend(0)
    else:
        i = pl.program_id(2)
        for ti in range(nq):
            pl.when(i == ti)(functools.partial(attend, ti))


def _full_blocks(i, tq, tk, start, past):
    first_chunk_start = ((start + i * tq) >> CHUNK_SHIFT) << CHUNK_SHIFT
    return (first_chunk_start - (start - past)) // tk


def _check_block_plan(nq, tq, tk, n_keys, start, past):
    for i in range(nq):
        n = _full_blocks(i, tq, tk, start, past)
        assert 0 <= n and (n + 1) * tk <= n_keys, (i, n)
        qc = (start + i * tq + np.arange(tq)) // CHUNK
        kc = (start - past + np.arange(n_keys)) // CHUNK
        vis = kc[None, :] <= qc[:, None]
        assert vis[:, :n * tk].all() and not vis[:, (n + 1) * tk:].any(), (i, n)
        assert vis[:, n * tk:(n + 1) * tk].any(axis=1).all()


def _attention(q, k, v, start, past):
    b, n_pairs, s, dq = q.shape
    n_heads, n_keys, dk = k.shape[1:]
    assert n_heads == 2 * n_pairs
    tq = _row_tile(s, ATTN_ROWS)
    ts = min(tq, ATTN_SUB_ROWS)
    tk = ts if n_keys % ts == 0 and s > ts else n_keys
    nq = s // tq
    hg = 4 if n_heads % 4 == 0 else 2
    _check_block_plan(s // ts, ts, tk, n_keys, start, past)
    return pl.pallas_call(
        functools.partial(_attn_kernel, hg=hg, tq=tq, ts=ts, tk=tk, nq=nq, start=start, past=past),
        out_shape=jax.ShapeDtypeStruct((b, s, n_heads * V_HEAD), BF16),
        grid=(b, n_heads // hg, nq),
        in_specs=[pl.BlockSpec((None, hg // 2, tq, dq), lambda bi, gi, i: (bi, gi, i, 0)),
                  pl.BlockSpec((None, hg, n_keys, dk), lambda bi, gi, i: (bi, gi, 0, 0)),
                  pl.BlockSpec((None, hg, n_keys, V_HEAD), lambda bi, gi, i: (bi, gi, 0, 0))],
        out_specs=pl.BlockSpec((None, tq, hg * V_HEAD), lambda bi, gi, i: (bi, i, gi)),
        scratch_shapes=[pltpu.VMEM((hg, n_keys, 2 * V_HEAD), BF16)],
        compiler_params=_params("parallel", "parallel", "arbitrary"),
        name="mla_attn",
    )(q, k, v)


def _twice(a):
    return jnp.concatenate([a, a], axis=-1)


_ROPE_SWAP = np.concatenate([np.arange(QK_ROPE // 2, QK_ROPE), np.arange(QK_ROPE // 2)])


def _rope_tables(start, s):
    inv = ROPE_BASE ** (-(jnp.arange(0, QK_ROPE, 2, dtype=F32) / QK_ROPE))
    ang = (start + jnp.arange(s, dtype=jnp.int32)).astype(F32)[:, None] * inv[None, :]
    cos, sin = jnp.cos(ang), jnp.sin(ang)
    return _twice(jnp.concatenate([cos, cos], axis=-1)), _twice(jnp.concatenate([-sin, sin], axis=-1))


def _prep_weights(p):
    n_heads = p["w_uk"].shape[1] // QK_NOPE
    assert n_heads % 2 == 0
    w = {"n_heads": n_heads}
    row = lambda a: a.reshape(1, -1).astype(F32)
    for name in ("ffn1_w_in", "ffn1_w_out", "ffn2_w_in", "ffn2_w_out", "pool_w", "w_dq", "w_o"):
        w[name] = p[name].astype(BF16)
    w["w_lat"] = jnp.concatenate(
        [p["w_dkv"], _twice(p["w_kr"]), _twice(p["w_kr"][:, _ROPE_SWAP])], axis=1).astype(BF16)
    w["kr_g"] = row(_twice(p["kr_norm"]))
    w["kr_gs"] = row(_twice(p["kr_norm"][_ROPE_SWAP]))
    w["w_kv"] = jnp.concatenate([p["w_uk"], p["w_uv"]], axis=1).astype(BF16)
    n_b, q_lora, _ = p["w_uq"].shape
    wq = p["w_uq"].reshape(n_b, q_lora, n_heads, QK_NOPE + QK_ROPE)
    nope, rope = wq[..., :QK_NOPE], wq[..., QK_NOPE:]
    flat = lambda a: a.reshape(n_b, q_lora, -1)
    w["w_q"] = jnp.concatenate([flat(nope), flat(rope), flat(rope[..., _ROPE_SWAP])], axis=-1).astype(BF16)
    w["qr_g"] = _twice(p["qr_norm"])
    w["qr_gs"] = _twice(p["qr_norm"][:, _ROPE_SWAP])
    return w


def _trunk(x, pool_prev, ckv_past, kr_past, start, p, w):
    b, s, d = x.shape
    past = 0 if ckv_past is None else ckv_past.shape[1]
    depth = p["ffn1_norm"].shape[0]
    n_a = p["pool_w"].shape[0]
    n_heads = w["n_heads"]
    row = lambda a: a.reshape(1, -1).astype(F32)
    cos, sin = _rope_tables(start, s)
    scale = float((QK_NOPE + QK_ROPE) ** -0.5 * np.log2(np.e))
    xf = x.reshape(b * s, d)
    new_pool = []
    for layer in range(depth):
        if layer == n_a:
            c_new, kr_new = _latent(xf, s, row(p["kv_norm"]), w["w_lat"], row(p["c_norm"]), w["kr_g"], w["kr_gs"],
                                    cos, sin)
            c_new = c_new.reshape(b, s, -1)
            kr_new = kr_new.reshape(b, s, LANES)
            c_all, kr_all = c_new, kr_new
            if past:
                c_all = jnp.concatenate([ckv_past.astype(F32), c_new], axis=1)
                kr_all = jnp.concatenate([_twice(kr_past.astype(F32)), kr_new], axis=1)
            keys, vals = _kv_expand(c_all, kr_all, w["w_kv"], row(p["kn_norm"]), n_heads)
        at = lambda name, idx=layer: _Stacked(w[name], idx)
        xf = _ffn(xf, row(p["ffn1_norm"][layer]), at("ffn1_w_in"), at("ffn1_w_out"))
        ffn2 = (row(p["ffn2_norm"][layer]), at("ffn2_w_in"), at("ffn2_w_out"))
        if layer < n_a:
            prev = jnp.zeros((b, POOL_HIST, d), F32) if pool_prev is None else \
                jnp.pad(pool_prev[layer].astype(F32), ((0, 0), (POOL_HIST - pool_prev.shape[2], 0), (0, 0)))
            fuse = s % FFN_ROWS == 0
            xm, st = _pool_mix(xf.reshape(b, s, d), prev, row(p["mix_norm"][layer]), at("pool_w"),
                               row(p["pool_scale"][layer]), start, ffn=ffn2 if fuse else None)
            new_pool.append(st[:, 1:])
            xf = xm.reshape(b * s, d) if fuse else _ffn(xm.reshape(b * s, d), *ffn2)
        else:
            i = layer - n_a
            q = _q_proj(xf.reshape(b, s, d), row(p["mix_norm"][layer]), at("w_dq", i), row(p["q_lat_norm"][i]),
                        at("w_q", i), row(p["qn_norm"][i]), row(w["qr_g"][i]), row(w["qr_gs"][i]), cos, sin,
                        n_heads, scale)
            att = _attention(q, keys, vals, start, past)
            xf = _ffn(xf, *ffn2, attn=att.reshape(b * s, -1), w_o=at("w_o", i))
    return xf.reshape(b, s, d), jnp.stack(new_pool, axis=0), c_new, kr_new[..., :QK_ROPE]


def kernel(x_prompt, x_sample, state_pool, cache_ckv, cache_krope, ffn1_norm, ffn1_w_in, ffn1_w_out, mix_norm, ffn2_norm, ffn2_w_in, ffn2_w_out, pool_w, pool_scale, kv_norm, w_dkv, c_norm, w_kr, kr_norm, w_uk, kn_norm, w_uv, w_dq, q_lat_norm, w_uq, qn_norm, qr_norm, w_o):
    p = dict(ffn1_norm=ffn1_norm, ffn1_w_in=ffn1_w_in, ffn1_w_out=ffn1_w_out, mix_norm=mix_norm,
             ffn2_norm=ffn2_norm, ffn2_w_in=ffn2_w_in, ffn2_w_out=ffn2_w_out, pool_w=pool_w,
             pool_scale=pool_scale, kv_norm=kv_norm, w_dkv=w_dkv, c_norm=c_norm, w_kr=w_kr, kr_norm=kr_norm,
             w_uk=w_uk, kn_norm=kn_norm, w_uv=w_uv, w_dq=w_dq, q_lat_norm=q_lat_norm, w_uq=w_uq,
             qn_norm=qn_norm, qr_norm=qr_norm, w_o=w_o)
    w = _prep_weights(p)
    y_p, pool_p, ckv_p, kr_p = _trunk(x_prompt, None, None, None, 0, p, w)
    y_s, pool_s, ckv_s, kr_s = _trunk(x_sample, state_pool, cache_ckv, cache_krope, cache_ckv.shape[1], p, w)
    return (y_p, y_s, pool_p, pool_s, ckv_p, kr_p, ckv_s, kr_s)
```

```python
import functools
from typing import NamedTuple

import numpy as np
import jax
import jax.numpy as jnp
from jax import lax
from jax.experimental import pallas as pl
from jax.experimental.pallas import tpu as pltpu

F32 = jnp.float32
BF16 = jnp.bfloat16

EPS = 1e-6
CHUNK = 64
CHUNK_SHIFT = 6
ROPE_BASE = 10000.0
POOL_WINDOWS = (2, 4, 8, 16)
POOL_HIST = 16
QK_NOPE = 128
QK_ROPE = 64
V_HEAD = 128
LANES = 128
VMEM_LIMIT_BYTES = 56 * 1024 * 1024
MXU_COLS = 256
FFN_ROWS = 1024
FFN_CHUNKS = 2
ATTN_ROWS = 512
ATTN_SUB_ROWS = 256
STREAM_ROWS = 1024


def _rms(x, g, n=None):
    n = x.shape[-1] if n is None else n
    ms = jnp.sum(x * x, axis=-1, keepdims=True) * (1.0 / n)
    return x * lax.rsqrt(ms + EPS) * g


class _Stacked(NamedTuple):
    stack: jax.Array
    idx: int

    @property
    def shape(self):
        return self.stack.shape[1:]


def _arr(a):
    return a.stack if isinstance(a, _Stacked) else a


def _resident(a):
    shape = a if isinstance(a, tuple) and not isinstance(a, _Stacked) else tuple(a.shape)
    zeros = (0,) * len(shape)
    if isinstance(a, _Stacked):
        return pl.BlockSpec((None,) + shape, lambda *_: (a.idx,) + zeros, pipeline_mode=pl.Buffered(1))
    return pl.BlockSpec(shape, lambda *_: zeros, pipeline_mode=pl.Buffered(1))


def _params(*sem):
    return pltpu.CompilerParams(dimension_semantics=sem, vmem_limit_bytes=VMEM_LIMIT_BYTES)


def _row_tile(t, cap=FFN_ROWS):
    tm = cap
    while t % tm:
        tm //= 2
    assert tm >= 8, (t, cap)
    return tm


def _ffn_chunks(d_ff):
    units = d_ff // MXU_COLS
    assert units * MXU_COLS == d_ff
    n = min(FFN_CHUNKS, units)
    sizes = [(units + n - 1 - i) // n * MXU_COLS for i in range(n)]
    return [(sum(sizes[:i]), sizes[i]) for i in range(n)]


def _ffn_body(x, g_ref, win_ref, wout_ref, d_ff):
    h = _rms(x, g_ref[...]).astype(BF16)
    y = None
    for c0, wc in _ffn_chunks(d_ff):
        gate = jnp.dot(h, win_ref[:, c0:c0 + wc], preferred_element_type=F32)
        up = jnp.dot(h, win_ref[:, d_ff + c0:d_ff + c0 + wc], preferred_element_type=F32)
        a = (jax.nn.silu(gate) * up).astype(BF16)
        yc = jnp.dot(a, wout_ref[c0:c0 + wc, :], preferred_element_type=F32)
        y = yc if y is None else y + yc
    return x + 0.5 * y


def _ffn_kernel(x_ref, g_ref, win_ref, wout_ref, o_ref, *, d_ff):
    o_ref[...] = _ffn_body(x_ref[...], g_ref, win_ref, wout_ref, d_ff)


def _ffn_wo_kernel(x_ref, a_ref, wo_ref, g_ref, win_ref, wout_ref, o_ref, *, d_ff):
    x = x_ref[...] + jnp.dot(a_ref[...], wo_ref[...], preferred_element_type=F32)
    o_ref[...] = _ffn_body(x, g_ref, win_ref, wout_ref, d_ff)


def _ffn(x, g, w_in, w_out, attn=None, w_o=None):
    t, d = x.shape
    d_ff = w_out.shape[0]
    tm = _row_tile(t)
    row = pl.BlockSpec((tm, d), lambda i: (i, 0))
    w_specs = [_resident((1, d)), _resident(w_in), _resident(w_out)]
    if attn is None:
        kern, ins, specs = _ffn_kernel, (x, g, w_in, w_out), [row] + w_specs
    else:
        kern = _ffn_wo_kernel
        ins = (x, attn, w_o, g, w_in, w_out)
        specs = [row, pl.BlockSpec((tm, attn.shape[1]), lambda i: (i, 0)), _resident(w_o)] + w_specs
    return pl.pallas_call(
        functools.partial(kern, d_ff=d_ff),
        out_shape=jax.ShapeDtypeStruct((t, d), F32),
        grid=(t // tm,),
        in_specs=specs,
        out_specs=row,
        compiler_params=_params("parallel"),
        name="ffn" if attn is None else "ffn_wo",
    )(*map(_arr, ins))


def _pool_reset(j, prev_ref, carry_ref, start):
    @pl.when(j == 0)
    def _():
        prev = prev_ref[...]
        if start < POOL_HIST:
            rowpos = start - POOL_HIST + lax.broadcasted_iota(jnp.int32, (POOL_HIST, 1), 0)
            prev = jnp.where(rowpos >= 0, prev, 0.0)
        carry_ref[...] = prev


def _pool_deltas(j, x_ref, g_ref, st_ref, carry_ref, start, ts, group):
    h = _rms(x_ref[...], g_ref[...])
    ext = jnp.concatenate([carry_ref[...], h], axis=0)
    carry_ref[...] = h[ts - POOL_HIST:, :]
    st_ref[...] = h[ts - POOL_HIST:, :]
    pos = start + j * ts + lax.broadcasted_iota(jnp.int32, (ts, 1), 0)
    outs = []
    for gi, w in enumerate(POOL_WINDOWS):
        lo, hi = gi * group, (gi + 1) * group
        p = ext[:, lo:hi]
        k = 1
        while k < w:
            p = p + pltpu.roll(p, k, axis=0)
            k *= 2
        inv_cnt = 1.0 / jnp.minimum(pos + 1, w).astype(F32)
        outs.append((p[POOL_HIST:, :] * inv_cnt - h[:, lo:hi]).astype(BF16))
    return jnp.concatenate(outs, axis=-1)


def _pool_apply(x, dlt, w_ref, sc_ref):
    group = w_ref.shape[1]
    outs = [jnp.dot(dlt[:, gi * group:(gi + 1) * group], w_ref[gi], preferred_element_type=F32)
            for gi in range(len(POOL_WINDOWS))]
    return x + jnp.concatenate(outs, axis=-1) * sc_ref[...]


def _pool_kernel(x_ref, prev_ref, g_ref, w_ref, sc_ref, o_ref, st_ref, carry_ref, *, start, ts):
    j = pl.program_id(1)
    _pool_reset(j, prev_ref, carry_ref, start)
    dlt = _pool_deltas(j, x_ref, g_ref, st_ref, carry_ref, start, ts, w_ref.shape[1])
    o_ref[...] = _pool_apply(x_ref[...], dlt, w_ref, sc_ref)


def _pool_ffn_kernel(x_ref, prev_ref, g_ref, w_ref, sc_ref, fg_ref, win_ref, wout_ref, o_ref, st_ref, carry_ref,
                     *, start, ts, d_ff):
    j = pl.program_id(1)
    _pool_reset(j, prev_ref, carry_ref, start)
    dlt = _pool_deltas(j, x_ref, g_ref, st_ref, carry_ref, start, ts, w_ref.shape[1])
    xm = _pool_apply(x_ref[...], dlt, w_ref, sc_ref)
    o_ref[...] = _ffn_body(xm, fg_ref, win_ref, wout_ref, d_ff)


def _pool_mix(x, prev16, g, w_pool, scale, start, ffn=None):
    b, s, d = x.shape
    ts = _row_tile(s)
    assert ts >= POOL_HIST and len(POOL_WINDOWS) * w_pool.shape[1] == d
    seq = pl.BlockSpec((None, ts, d), lambda bi, j: (bi, j, 0))
    hist = pl.BlockSpec((None, POOL_HIST, d), lambda bi, j: (bi, 0, 0))
    ins = [x, prev16, g, w_pool, scale]
    specs = [seq, hist, _resident((1, d)), _resident(w_pool), _resident((1, d))]
    if ffn is None:
        kern = functools.partial(_pool_kernel, start=start, ts=ts)
    else:
        kern = functools.partial(_pool_ffn_kernel, start=start, ts=ts, d_ff=ffn[2].shape[0])
        ins += list(ffn)
        specs += [_resident((1, d)), _resident(ffn[1]), _resident(ffn[2])]
    return pl.pallas_call(
        kern,
        out_shape=(jax.ShapeDtypeStruct((b, s, d), F32), jax.ShapeDtypeStruct((b, POOL_HIST, d), F32)),
        grid=(b, s // ts),
        in_specs=specs,
        out_specs=(seq, hist),
        scratch_shapes=[pltpu.VMEM((POOL_HIST, d), F32)],
        compiler_params=_params("parallel", "arbitrary"),
        name="pool_mix" if ffn is None else "pool_ffn",
    )(*map(_arr, ins))


def _rope_rows(z, z_sw, g, g_sw, cos, sin, same_halves):
    zz = z * z
    if same_halves:
        r = lax.rsqrt(jnp.sum(zz, axis=-1, keepdims=True) * (1.0 / LANES) + EPS)
    else:
        lo = lax.broadcasted_iota(jnp.int32, (1, LANES), 1) < QK_ROPE
        r_lo = lax.rsqrt(jnp.sum(jnp.where(lo, zz, 0.0), axis=-1, keepdims=True) * (1.0 / QK_ROPE) + EPS)
        r_hi = lax.rsqrt(jnp.sum(jnp.where(lo, 0.0, zz), axis=-1, keepdims=True) * (1.0 / QK_ROPE) + EPS)
        r = jnp.where(lo, r_lo, r_hi)
    return (z * r) * g * cos + (z_sw * r) * g_sw * sin


def _latent_kernel(x_ref, g_ref, w_ref, cg_ref, rg_ref, rgs_ref, cos_ref, sin_ref, c_ref, kr_ref, *, kv_lora):
    h = _rms(x_ref[...], g_ref[...]).astype(BF16)
    z = jnp.dot(h, w_ref[...], preferred_element_type=F32)
    c_ref[...] = _rms(z[:, :kv_lora], cg_ref[...])
    kr_ref[...] = _rope_rows(z[:, kv_lora:kv_lora + LANES], z[:, kv_lora + LANES:],
                             rg_ref[...], rgs_ref[...], cos_ref[...], sin_ref[...], same_halves=True)


def _table_spec(tm, s):
    if tm <= s:
        assert s % tm == 0
        per = s // tm
        return pl.BlockSpec((tm, LANES), lambda i: (i % per, 0))
    return pl.BlockSpec((tm, LANES), lambda i: (0, 0))


def _tables_for(cos, sin, tm, s):
    if tm <= s:
        return cos, sin
    assert tm % s == 0
    return jnp.tile(cos, (tm // s, 1)), jnp.tile(sin, (tm // s, 1))


def _latent(x, s, g, w_lat, c_g, r_g, r_gs, cos, sin):
    t, d = x.shape
    kv_lora = c_g.shape[1]
    tm = _row_tile(t, STREAM_ROWS)
    cos_t, sin_t = _tables_for(cos, sin, tm, s)
    return pl.pallas_call(
        functools.partial(_latent_kernel, kv_lora=kv_lora),
        out_shape=(jax.ShapeDtypeStruct((t, kv_lora), F32), jax.ShapeDtypeStruct((t, LANES), F32)),
        grid=(t // tm,),
        in_specs=[pl.BlockSpec((tm, d), lambda i: (i, 0)), _resident((1, d)), _resident(w_lat.shape),
                  _resident((1, kv_lora)), _resident((1, LANES)), _resident((1, LANES)),
                  _table_spec(tm, s), _table_spec(tm, s)],
        out_specs=(pl.BlockSpec((tm, kv_lora), lambda i: (i, 0)), pl.BlockSpec((tm, LANES), lambda i: (i, 0))),
        compiler_params=_params("parallel"),
        name="mla_latent",
    )(x, g, w_lat, c_g, r_g, r_gs, cos_t, sin_t)


def _kv_kernel(c_ref, kr_ref, w_ref, g_ref, k_ref, v_ref, *, n_heads):
    z = jnp.dot(c_ref[...].astype(BF16), w_ref[...], preferred_element_type=F32)
    lo = lax.broadcasted_iota(jnp.int32, (1, LANES), 1) < QK_ROPE
    kr = kr_ref[...]
    kr_half = (jnp.where(lo, kr, 0.0).astype(BF16), jnp.where(lo, 0.0, kr).astype(BF16))
    for hh in range(n_heads):
        kn = _rms(z[:, hh * QK_NOPE:(hh + 1) * QK_NOPE], g_ref[...]).astype(BF16)
        k_ref[hh] = jnp.concatenate([kn, kr_half[hh % 2]], axis=-1)
        off = n_heads * QK_NOPE + hh * V_HEAD
        v_ref[hh] = z[:, off:off + V_HEAD].astype(BF16)


def _kv_expand(c_all, kr_all, w_kv, kn_g, n_heads):
    b, tk, kv_lora = c_all.shape
    tr = tk if tk % STREAM_ROWS else STREAM_ROWS
    return pl.pallas_call(
        functools.partial(_kv_kernel, n_heads=n_heads),
        out_shape=(jax.ShapeDtypeStruct((b, n_heads, tk, QK_NOPE + LANES), BF16),
                   jax.ShapeDtypeStruct((b, n_heads, tk, V_HEAD), BF16)),
        grid=(b, tk // tr),
        in_specs=[pl.BlockSpec((None, tr, kv_lora), lambda bi, j: (bi, j, 0)),
                  pl.BlockSpec((None, tr, LANES), lambda bi, j: (bi, j, 0)),
                  _resident(w_kv.shape), _resident((1, QK_NOPE))],
        out_specs=(pl.BlockSpec((None, n_heads, tr, QK_NOPE + LANES), lambda bi, j: (bi, 0, j, 0)),
                   pl.BlockSpec((None, n_heads, tr, V_HEAD), lambda bi, j: (bi, 0, j, 0))),
        compiler_params=_params("parallel", "parallel"),
        name="mla_kv",
    )(c_all, kr_all, w_kv, kn_g)


def _q_kernel(x_ref, g_ref, wdq_ref, lg_ref, wq_ref, ng_ref, rg_ref, rgs_ref, cos_ref, sin_ref, q_ref,
              *, n_heads):
    h = _rms(x_ref[...], g_ref[...]).astype(BF16)
    ql = _rms(jnp.dot(h, wdq_ref[...], preferred_element_type=F32), lg_ref[...]).astype(BF16)
    z = jnp.dot(ql, wq_ref[...], preferred_element_type=F32)
    cos, sin = cos_ref[...], sin_ref[...]
    n_pairs = n_heads // 2
    for pp in range(n_pairs):
        qn = [_rms(z[:, hh * QK_NOPE:(hh + 1) * QK_NOPE], ng_ref[...]) for hh in (2 * pp, 2 * pp + 1)]
        o1 = n_heads * QK_NOPE + pp * LANES
        o2 = o1 + n_pairs * LANES
        qr = _rope_rows(z[:, o1:o1 + LANES], z[:, o2:o2 + LANES], rg_ref[...], rgs_ref[...], cos, sin,
                        same_halves=False)
        q_ref[pp] = jnp.concatenate(qn + [qr], axis=-1).astype(BF16)


def _q_proj(x, g, w_dq, l_g, w_q, n_g, r_g, r_gs, cos, sin, n_heads, scale):
    b, s, d = x.shape
    tm = _row_tile(s, STREAM_ROWS)
    q_lora = w_dq.shape[1]
    n_pairs = n_heads // 2
    dq = 2 * QK_NOPE + LANES
    tab = pl.BlockSpec((tm, LANES), lambda bi, j: (j, 0))
    return pl.pallas_call(
        functools.partial(_q_kernel, n_heads=n_heads),
        out_shape=jax.ShapeDtypeStruct((b, n_pairs, s, dq), BF16),
        grid=(b, s // tm),
        in_specs=[pl.BlockSpec((None, tm, d), lambda bi, j: (bi, j, 0)), _resident((1, d)),
                  _resident(w_dq), _resident((1, q_lora)), _resident(w_q),
                  _resident((1, QK_NOPE)), _resident((1, LANES)), _resident((1, LANES)), tab, tab],
        out_specs=pl.BlockSpec((None, n_pairs, tm, dq), lambda bi, j: (bi, 0, j, 0)),
        compiler_params=_params("parallel", "parallel"),
        name="mla_q",
    )(x, g, _arr(w_dq), l_g, _arr(w_q), n_g * scale, r_g * scale, r_gs * scale, cos, sin)


def _attn_kernel(q_ref, k_ref, v_ref, o_ref, vx_ref, *, hg, tq, ts, tk, nq, start, past):
    @pl.when(pl.program_id(2) == 0)
    def _():
        vx_ref[:, :, :V_HEAD] = v_ref[...]
        vx_ref[:, :, V_HEAD:] = jnp.ones(v_ref.shape, BF16)

    def attend(ti):
        chains = []
        for sub in range(tq // ts):
            si = ti * (tq // ts) + sub
            nk = (_full_blocks(si, ts, tk, start, past) + 1) * tk
            qpos = start + si * ts + lax.broadcasted_iota(jnp.int32, (ts, 1), 0)
            kpos = start - past + (nk - tk) + lax.broadcasted_iota(jnp.int32, (1, tk), 1)
            visible = jnp.right_shift(kpos, CHUNK_SHIFT) <= jnp.right_shift(qpos, CHUNK_SHIFT)
            chains += [(sub * ts, hh, nk, visible) for hh in range(hg)]
        scores, probs = [], []
        for r0, hh, nk, _ in chains:
            lo = (hh % 2) * QK_NOPE
            q = jnp.concatenate([q_ref[hh // 2, r0:r0 + ts, lo:lo + QK_NOPE],
                                 q_ref[hh // 2, r0:r0 + ts, 2 * QK_NOPE:]], axis=-1)
            scores.append(lax.dot_general(q, k_ref[hh, :nk, :], (((1,), (1,)), ((), ())),
                                          preferred_element_type=F32))
        for (_, _, nk, visible), s in zip(chains, scores):
            last = jnp.where(visible, s[:, nk - tk:], -jnp.inf)
            m = jnp.max(last, axis=-1, keepdims=True)
            if nk > tk:
                m = jnp.maximum(m, jnp.max(s[:, :nk - tk], axis=-1, keepdims=True))
                p = jnp.concatenate([jnp.exp2(s[:, :nk - tk] - m), jnp.exp2(last - m)], axis=-1)
            else:
                p = jnp.exp2(last - m)
            probs.append(p.astype(BF16))
        for (r0, hh, nk, _), p in zip(chains, probs):
            pv = jnp.dot(p, vx_ref[hh, :nk, :], preferred_element_type=F32)
            o_ref[r0:r0 + ts, hh * V_HEAD:(hh + 1) * V_HEAD] = (pv[:, :V_HEAD] / pv[:, V_HEAD:]).astype(BF16)

    if nq == 1:
        attend(0)
    else:
        i = pl.program_id(2)
        for ti in range(nq):
            pl.when(i == ti)(functools.partial(attend, ti))


def _full_blocks(i, tq, tk, start, past):
    first_chunk_start = ((start + i * tq) >> CHUNK_SHIFT) << CHUNK_SHIFT
    return (first_chunk_start - (start - past)) // tk


def _check_block_plan(nq, tq, tk, n_keys, start, past):
    for i in range(nq):
        n = _full_blocks(i, tq, tk, start, past)
        assert 0 <= n and (n + 1) * tk <= n_keys, (i, n)
        qc = (start + i * tq + np.arange(tq)) // CHUNK
        kc = (start - past + np.arange(n_keys)) // CHUNK
        vis = kc[None, :] <= qc[:, None]
        assert vis[:, :n * tk].all() and not vis[:, (n + 1) * tk:].any(), (i, n)
        assert vis[:, n * tk:(n + 1) * tk].any(axis=1).all()


def _attention(q, k, v, start, past):
    b, n_pairs, s, dq = q.shape
    n_heads, n_keys, dk = k.shape[1:]
    assert n_heads == 2 * n_pairs
    tq = _row_tile(s, ATTN_ROWS)
    ts = min(tq, ATTN_SUB_ROWS)
    tk = ts if n_keys % ts == 0 and s > ts else n_keys
    nq = s // tq
    hg = 4 if n_heads % 4 == 0 else 2
    _check_block_plan(s // ts, ts, tk, n_keys, start, past)
    return pl.pallas_call(
        functools.partial(_attn_kernel, hg=hg, tq=tq, ts=ts, tk=tk, nq=nq, start=start, past=past),
        out_shape=jax.ShapeDtypeStruct((b, s, n_heads * V_HEAD), BF16),
        grid=(b, n_heads // hg, nq),
        in_specs=[pl.BlockSpec((None, hg // 2, tq, dq), lambda bi, gi, i: (bi, gi, i, 0)),
                  pl.BlockSpec((None, hg, n_keys, dk), lambda bi, gi, i: (bi, gi, 0, 0)),
                  pl.BlockSpec((None, hg, n_keys, V_HEAD), lambda bi, gi, i: (bi, gi, 0, 0))],
        out_specs=pl.BlockSpec((None, tq, hg * V_HEAD), lambda bi, gi, i: (bi, i, gi)),
        scratch_shapes=[pltpu.VMEM((hg, n_keys, 2 * V_HEAD), BF16)],
        compiler_params=_params("parallel", "parallel", "arbitrary"),
        name="mla_attn",
    )(q, k, v)


def _twice(a):
    return jnp.concatenate([a, a], axis=-1)


_ROPE_SWAP = np.concatenate([np.arange(QK_ROPE // 2, QK_ROPE), np.arange(QK_ROPE // 2)])


def _rope_tables(start, s):
    inv = ROPE_BASE ** (-(jnp.arange(0, QK_ROPE, 2, dtype=F32) / QK_ROPE))
    ang = (start + jnp.arange(s, dtype=jnp.int32)).astype(F32)[:, None] * inv[None, :]
    cos, sin = jnp.cos(ang), jnp.sin(ang)
    return _twice(jnp.concatenate([cos, cos], axis=-1)), _twice(jnp.concatenate([-sin, sin], axis=-1))


def _prep_weights(p):
    n_heads = p["w_uk"].shape[1] // QK_NOPE
    assert n_heads % 2 == 0
    w = {"n_heads": n_heads}
    row = lambda a: a.reshape(1, -1).astype(F32)
    for name in ("ffn1_w_in", "ffn1_w_out", "ffn2_w_in", "ffn2_w_out", "pool_w", "w_dq", "w_o"):
        w[name] = p[name].astype(BF16)
    w["w_lat"] = jnp.concatenate(
        [p["w_dkv"], _twice(p["w_kr"]), _twice(p["w_kr"][:, _ROPE_SWAP])], axis=1).astype(BF16)
    w["kr_g"] = row(_twice(p["kr_norm"]))
    w["kr_gs"] = row(_twice(p["kr_norm"][_ROPE_SWAP]))
    w["w_kv"] = jnp.concatenate([p["w_uk"], p["w_uv"]], axis=1).astype(BF16)
    n_b, q_lora, _ = p["w_uq"].shape
    wq = p["w_uq"].reshape(n_b, q_lora, n_heads, QK_NOPE + QK_ROPE)
    nope, rope = wq[..., :QK_NOPE], wq[..., QK_NOPE:]
    flat = lambda a: a.reshape(n_b, q_lora, -1)
    w["w_q"] = jnp.concatenate([flat(nope), flat(rope), flat(rope[..., _ROPE_SWAP])], axis=-1).astype(BF16)
    w["qr_g"] = _twice(p["qr_norm"])
    w["qr_gs"] = _twice(p["qr_norm"][:, _ROPE_SWAP])
    return w


def _trunk(x, pool_prev, ckv_past, kr_past, start, p, w):
    b, s, d = x.shape
    past = 0 if ckv_past is None else ckv_past.shape[1]
    depth = p["ffn1_norm"].shape[0]
    n_a = p["pool_w"].shape[0]
    n_heads = w["n_heads"]
    row = lambda a: a.reshape(1, -1).astype(F32)
    cos, sin = _rope_tables(start, s)
    scale = float((QK_NOPE + QK_ROPE) ** -0.5 * np.log2(np.e))
    xf = x.reshape(b * s, d)
    new_pool = []
    for layer in range(depth):
        if layer == n_a:
            c_new, kr_new = _latent(xf, s, row(p["kv_norm"]), w["w_lat"], row(p["c_norm"]), w["kr_g"], w["kr_gs"],
                                    cos, sin)
            c_new = c_new.reshape(b, s, -1)
            kr_new = kr_new.reshape(b, s, LANES)
            c_all, kr_all = c_new, kr_new
            if past:
                c_all = jnp.concatenate([ckv_past.astype(F32), c_new], axis=1)
                kr_all = jnp.concatenate([_twice(kr_past.astype(F32)), kr_new], axis=1)
            keys, vals = _kv_expand(c_all, kr_all, w["w_kv"], row(p["kn_norm"]), n_heads)
        at = lambda name, idx=layer: _Stacked(w[name], idx)
        xf = _ffn(xf, row(p["ffn1_norm"][layer]), at("ffn1_w_in"), at("ffn1_w_out"))
        ffn2 = (row(p["ffn2_norm"][layer]), at("ffn2_w_in"), at("ffn2_w_out"))
        if layer < n_a:
            prev = jnp.zeros((b, POOL_HIST, d), F32) if pool_prev is None else \
                jnp.pad(pool_prev[layer].astype(F32), ((0, 0), (POOL_HIST - pool_prev.shape[2], 0), (0, 0)))
            fuse = s % FFN_ROWS == 0
            xm, st = _pool_mix(xf.reshape(b, s, d), prev, row(p["mix_norm"][layer]), at("pool_w"),
                               row(p["pool_scale"][layer]), start, ffn=ffn2 if fuse else None)
            new_pool.append(st[:, 1:])
            xf = xm.reshape(b * s, d) if fuse else _ffn(xm.reshape(b * s, d), *ffn2)
        else:
            i = layer - n_a
            q = _q_proj(xf.reshape(b, s, d), row(p["mix_norm"][layer]), at("w_dq", i), row(p["q_lat_norm"][i]),
                        at("w_q", i), row(p["qn_norm"][i]), row(w["qr_g"][i]), row(w["qr_gs"][i]), cos, sin,
                        n_heads, scale)
            att = _attention(q, keys, vals, start, past)
            xf = _ffn(xf, *ffn2, attn=att.reshape(b * s, -1), w_o=at("w_o", i))
    return xf.reshape(b, s, d), jnp.stack(new_pool, axis=0), c_new, kr_new[..., :QK_ROPE]


def kernel(x_prompt, x_sample, state_pool, cache_ckv, cache_krope, ffn1_norm, ffn1_w_in, ffn1_w_out, mix_norm, ffn2_norm, ffn2_w_in, ffn2_w_out, pool_w, pool_scale, kv_norm, w_dkv, c_norm, w_kr, kr_norm, w_uk, kn_norm, w_uv, w_dq, q_lat_norm, w_uq, qn_norm, qr_norm, w_o):
    p = dict(ffn1_norm=ffn1_norm, ffn1_w_in=ffn1_w_in, ffn1_w_out=ffn1_w_out, mix_norm=mix_norm,
             ffn2_norm=ffn2_norm, ffn2_w_in=ffn2_w_in, ffn2_w_out=ffn2_w_out, pool_w=pool_w,
             pool_scale=pool_scale, kv_norm=kv_norm, w_dkv=w_dkv, c_norm=c_norm, w_kr=w_kr, kr_norm=kr_norm,
             w_uk=w_uk, kn_norm=kn_norm, w_uv=w_uv, w_dq=w_dq, q_lat_norm=q_lat_norm, w_uq=w_uq,
             qn_norm=qn_norm, qr_norm=qr_norm, w_o=w_o)
    w = _prep_weights(p)
    y_p, pool_p, ckv_p, kr_p = _trunk(x_prompt, None, None, None, 0, p, w)
    y_s, pool_s, ckv_s, kr_s = _trunk(x_sample, state_pool, cache_ckv, cache_krope, cache_ckv.shape[1], p, w)
    return (y_p, y_s, pool_p, pool_s, ckv_p, kr_p, ckv_s, kr_s)
```

```python
import functools
from typing import NamedTuple

import numpy as np
import jax
import jax.numpy as jnp
from jax import lax
from jax.experimental import pallas as pl
from jax.experimental.pallas import tpu as pltpu

F32 = jnp.float32
BF16 = jnp.bfloat16

EPS = 1e-6
CHUNK = 64
CHUNK_SHIFT = 6
ROPE_BASE = 10000.0
POOL_WINDOWS = (2, 4, 8, 16)
POOL_HIST = 16
QK_NOPE = 128
QK_ROPE = 64
V_HEAD = 128
LANES = 128
VMEM_LIMIT_BYTES = 56 * 1024 * 1024
MXU_COLS = 256
FFN_ROWS = 1024
FFN_CHUNKS = 2
ATTN_ROWS = 1024
ATTN_SUB_ROWS = 256
STREAM_ROWS = 1024


def _rms(x, g, n=None):
    n = x.shape[-1] if n is None else n
    ms = jnp.sum(x * x, axis=-1, keepdims=True) * (1.0 / n)
    return x * lax.rsqrt(ms + EPS) * g


class _Stacked(NamedTuple):
    stack: jax.Array
    idx: int

    @property
    def shape(self):
        return self.stack.shape[1:]


def _arr(a):
    return a.stack if isinstance(a, _Stacked) else a


def _resident(a):
    shape = a if isinstance(a, tuple) and not isinstance(a, _Stacked) else tuple(a.shape)
    zeros = (0,) * len(shape)
    if isinstance(a, _Stacked):
        return pl.BlockSpec((None,) + shape, lambda *_: (a.idx,) + zeros, pipeline_mode=pl.Buffered(1))
    return pl.BlockSpec(shape, lambda *_: zeros, pipeline_mode=pl.Buffered(1))


def _params(*sem):
    return pltpu.CompilerParams(dimension_semantics=sem, vmem_limit_bytes=VMEM_LIMIT_BYTES)


def _row_tile(t, cap=FFN_ROWS):
    tm = cap
    while t % tm:
        tm //= 2
    assert tm >= 8, (t, cap)
    return tm


def _ffn_chunks(d_ff):
    units = d_ff // MXU_COLS
    assert units * MXU_COLS == d_ff
    n = min(FFN_CHUNKS, units)
    sizes = [(units + n - 1 - i) // n * MXU_COLS for i in range(n)]
    return [(sum(sizes[:i]), sizes[i]) for i in range(n)]


def _ffn_body(x, g_ref, win_ref, wout_ref, d_ff):
    h = _rms(x, g_ref[...]).astype(BF16)
    y = None
    for c0, wc in _ffn_chunks(d_ff):
        gate = jnp.dot(h, win_ref[:, c0:c0 + wc], preferred_element_type=F32)
        up = jnp.dot(h, win_ref[:, d_ff + c0:d_ff + c0 + wc], preferred_element_type=F32)
        a = (jax.nn.silu(gate) * up).astype(BF16)
        yc = jnp.dot(a, wout_ref[c0:c0 + wc, :], preferred_element_type=F32)
        y = yc if y is None else y + yc
    return x + 0.5 * y


def _ffn_kernel(x_ref, g_ref, win_ref, wout_ref, o_ref, *, d_ff):
    o_ref[...] = _ffn_body(x_ref[...], g_ref, win_ref, wout_ref, d_ff)


def _ffn_wo_kernel(x_ref, a_ref, wo_ref, g_ref, win_ref, wout_ref, o_ref, *, d_ff):
    x = x_ref[...] + jnp.dot(a_ref[...], wo_ref[...], preferred_element_type=F32)
    o_ref[...] = _ffn_body(x, g_ref, win_ref, wout_ref, d_ff)


def _ffn(x, g, w_in, w_out, attn=None, w_o=None):
    t, d = x.shape
    d_ff = w_out.shape[0]
    tm = _row_tile(t)
    row = pl.BlockSpec((tm, d), lambda i: (i, 0))
    w_specs = [_resident((1, d)), _resident(w_in), _resident(w_out)]
    if attn is None:
        kern, ins, specs = _ffn_kernel, (x, g, w_in, w_out), [row] + w_specs
    else:
        kern = _ffn_wo_kernel
        ins = (x, attn, w_o, g, w_in, w_out)
        specs = [row, pl.BlockSpec((tm, attn.shape[1]), lambda i: (i, 0)), _resident(w_o)] + w_specs
    return pl.pallas_call(
        functools.partial(kern, d_ff=d_ff),
        out_shape=jax.ShapeDtypeStruct((t, d), F32),
        grid=(t // tm,),
        in_specs=specs,
        out_specs=row,
        compiler_params=_params("parallel"),
        name="ffn" if attn is None else "ffn_wo",
    )(*map(_arr, ins))


def _pool_reset(j, prev_ref, carry_ref, start):
    @pl.when(j == 0)
    def _():
        prev = prev_ref[...]
        if start < POOL_HIST:
            rowpos = start - POOL_HIST + lax.broadcasted_iota(jnp.int32, (POOL_HIST, 1), 0)
            prev = jnp.where(rowpos >= 0, prev, 0.0)
        carry_ref[...] = prev


def _pool_deltas(j, x_ref, g_ref, st_ref, carry_ref, start, ts, group):
    h = _rms(x_ref[...], g_ref[...])
    ext = jnp.concatenate([carry_ref[...], h], axis=0)
    carry_ref[...] = h[ts - POOL_HIST:, :]
    st_ref[...] = h[ts - POOL_HIST:, :]
    pos = start + j * ts + lax.broadcasted_iota(jnp.int32, (ts, 1), 0)
    outs = []
    for gi, w in enumerate(POOL_WINDOWS):
        lo, hi = gi * group, (gi + 1) * group
        p = ext[:, lo:hi]
        k = 1
        while k < w:
            p = p + pltpu.roll(p, k, axis=0)
            k *= 2
        inv_cnt = 1.0 / jnp.minimum(pos + 1, w).astype(F32)
        outs.append((p[POOL_HIST:, :] * inv_cnt - h[:, lo:hi]).astype(BF16))
    return jnp.concatenate(outs, axis=-1)


def _pool_apply(x, dlt, w_ref, sc_ref):
    group = w_ref.shape[1]
    outs = [jnp.dot(dlt[:, gi * group:(gi + 1) * group], w_ref[gi], preferred_element_type=F32)
            for gi in range(len(POOL_WINDOWS))]
    return x + jnp.concatenate(outs, axis=-1) * sc_ref[...]


def _pool_kernel(x_ref, prev_ref, g_ref, w_ref, sc_ref, o_ref, st_ref, carry_ref, *, start, ts):
    j = pl.program_id(1)
    _pool_reset(j, prev_ref, carry_ref, start)
    dlt = _pool_deltas(j, x_ref, g_ref, st_ref, carry_ref, start, ts, w_ref.shape[1])
    o_ref[...] = _pool_apply(x_ref[...], dlt, w_ref, sc_ref)


def _pool_ffn_kernel(x_ref, prev_ref, g_ref, w_ref, sc_ref, fg_ref, win_ref, wout_ref, o_ref, st_ref, carry_ref,
                     *, start, ts, d_ff):
    j = pl.program_id(1)
    _pool_reset(j, prev_ref, carry_ref, start)
    dlt = _pool_deltas(j, x_ref, g_ref, st_ref, carry_ref, start, ts, w_ref.shape[1])
    xm = _pool_apply(x_ref[...], dlt, w_ref, sc_ref)
    o_ref[...] = _ffn_body(xm, fg_ref, win_ref, wout_ref, d_ff)


def _pool_mix(x, prev16, g, w_pool, scale, start, ffn=None):
    b, s, d = x.shape
    ts = _row_tile(s)
    assert ts >= POOL_HIST and len(POOL_WINDOWS) * w_pool.shape[1] == d
    seq = pl.BlockSpec((None, ts, d), lambda bi, j: (bi, j, 0))
    hist = pl.BlockSpec((None, POOL_HIST, d), lambda bi, j: (bi, 0, 0))
    ins = [x, prev16, g, w_pool, scale]
    specs = [seq, hist, _resident((1, d)), _resident(w_pool), _resident((1, d))]
    if ffn is None:
        kern = functools.partial(_pool_kernel, start=start, ts=ts)
    else:
        kern = functools.partial(_pool_ffn_kernel, start=start, ts=ts, d_ff=ffn[2].shape[0])
        ins += list(ffn)
        specs += [_resident((1, d)), _resident(ffn[1]), _resident(ffn[2])]
    return pl.pallas_call(
        kern,
        out_shape=(jax.ShapeDtypeStruct((b, s, d), F32), jax.ShapeDtypeStruct((b, POOL_HIST, d), F32)),
        grid=(b, s // ts),
        in_specs=specs,
        out_specs=(seq, hist),
        scratch_shapes=[pltpu.VMEM((POOL_HIST, d), F32)],
        compiler_params=_params("parallel", "arbitrary"),
        name="pool_mix" if ffn is None else "pool_ffn",
    )(*map(_arr, ins))


def _rope_rows(z, z_sw, g, g_sw, cos, sin, same_halves):
    zz = z * z
    if same_halves:
        r = lax.rsqrt(jnp.sum(zz, axis=-1, keepdims=True) * (1.0 / LANES) + EPS)
    else:
        lo = lax.broadcasted_iota(jnp.int32, (1, LANES), 1) < QK_ROPE
        r_lo = lax.rsqrt(jnp.sum(jnp.where(lo, zz, 0.0), axis=-1, keepdims=True) * (1.0 / QK_ROPE) + EPS)
        r_hi = lax.rsqrt(jnp.sum(jnp.where(lo, 0.0, zz), axis=-1, keepdims=True) * (1.0 / QK_ROPE) + EPS)
        r = jnp.where(lo, r_lo, r_hi)
    return (z * r) * g * cos + (z_sw * r) * g_sw * sin


def _latent_kernel(x_ref, g_ref, w_ref, cg_ref, rg_ref, rgs_ref, cos_ref, sin_ref, c_ref, kr_ref, *, kv_lora):
    h = _rms(x_ref[...], g_ref[...]).astype(BF16)
    z = jnp.dot(h, w_ref[...], preferred_element_type=F32)
    c_ref[...] = _rms(z[:, :kv_lora], cg_ref[...])
    kr_ref[...] = _rope_rows(z[:, kv_lora:kv_lora + LANES], z[:, kv_lora + LANES:],
                             rg_ref[...], rgs_ref[...], cos_ref[...], sin_ref[...], same_halves=True)


def _table_spec(tm, s):
    if tm <= s:
        assert s % tm == 0
        per = s // tm
        return pl.BlockSpec((tm, LANES), lambda i: (i % per, 0))
    return pl.BlockSpec((tm, LANES), lambda i: (0, 0))


def _tables_for(cos, sin, tm, s):
    if tm <= s:
        return cos, sin
    assert tm % s == 0
    return jnp.tile(cos, (tm // s, 1)), jnp.tile(sin, (tm // s, 1))


def _latent(x, s, g, w_lat, c_g, r_g, r_gs, cos, sin):
    t, d = x.shape
    kv_lora = c_g.shape[1]
    tm = _row_tile(t, STREAM_ROWS)
    cos_t, sin_t = _tables_for(cos, sin, tm, s)
    return pl.pallas_call(
        functools.partial(_latent_kernel, kv_lora=kv_lora),
        out_shape=(jax.ShapeDtypeStruct((t, kv_lora), F32), jax.ShapeDtypeStruct((t, LANES), F32)),
        grid=(t // tm,),
        in_specs=[pl.BlockSpec((tm, d), lambda i: (i, 0)), _resident((1, d)), _resident(w_lat.shape),
                  _resident((1, kv_lora)), _resident((1, LANES)), _resident((1, LANES)),
                  _table_spec(tm, s), _table_spec(tm, s)],
        out_specs=(pl.BlockSpec((tm, kv_lora), lambda i: (i, 0)), pl.BlockSpec((tm, LANES), lambda i: (i, 0))),
        compiler_params=_params("parallel"),
        name="mla_latent",
    )(x, g, w_lat, c_g, r_g, r_gs, cos_t, sin_t)


def _kv_kernel(c_ref, kr_ref, w_ref, g_ref, k_ref, v_ref, *, n_heads):
    z = jnp.dot(c_ref[...].astype(BF16), w_ref[...], preferred_element_type=F32)
    lo = lax.broadcasted_iota(jnp.int32, (1, LANES), 1) < QK_ROPE
    kr = kr_ref[...]
    kr_half = (jnp.where(lo, kr, 0.0).astype(BF16), jnp.where(lo, 0.0, kr).astype(BF16))
    for hh in range(n_heads):
        kn = _rms(z[:, hh * QK_NOPE:(hh + 1) * QK_NOPE], g_ref[...]).astype(BF16)
        k_ref[hh] = jnp.concatenate([kn, kr_half[hh % 2]], axis=-1)
        off = n_heads * QK_NOPE + hh * V_HEAD
        v_ref[hh] = z[:, off:off + V_HEAD].astype(BF16)


def _kv_expand(c_all, kr_all, w_kv, kn_g, n_heads):
    b, tk, kv_lora = c_all.shape
    tr = tk if tk % STREAM_ROWS else STREAM_ROWS
    return pl.pallas_call(
        functools.partial(_kv_kernel, n_heads=n_heads),
        out_shape=(jax.ShapeDtypeStruct((b, n_heads, tk, QK_NOPE + LANES), BF16),
                   jax.ShapeDtypeStruct((b, n_heads, tk, V_HEAD), BF16)),
        grid=(b, tk // tr),
        in_specs=[pl.BlockSpec((None, tr, kv_lora), lambda bi, j: (bi, j, 0)),
                  pl.BlockSpec((None, tr, LANES), lambda bi, j: (bi, j, 0)),
                  _resident(w_kv.shape), _resident((1, QK_NOPE))],
        out_specs=(pl.BlockSpec((None, n_heads, tr, QK_NOPE + LANES), lambda bi, j: (bi, 0, j, 0)),
                   pl.BlockSpec((None, n_heads, tr, V_HEAD), lambda bi, j: (bi, 0, j, 0))),
        compiler_params=_params("parallel", "parallel"),
        name="mla_kv",
    )(c_all, kr_all, w_kv, kn_g)


def _q_kernel(x_ref, g_ref, wdq_ref, lg_ref, wq_ref, ng_ref, rg_ref, rgs_ref, cos_ref, sin_ref, q_ref,
              *, n_heads):
    h = _rms(x_ref[...], g_ref[...]).astype(BF16)
    ql = _rms(jnp.dot(h, wdq_ref[...], preferred_element_type=F32), lg_ref[...]).astype(BF16)
    z = jnp.dot(ql, wq_ref[...], preferred_element_type=F32)
    cos, sin = cos_ref[...], sin_ref[...]
    n_pairs = n_heads // 2
    for pp in range(n_pairs):
        qn = [_rms(z[:, hh * QK_NOPE:(hh + 1) * QK_NOPE], ng_ref[...]) for hh in (2 * pp, 2 * pp + 1)]
        o1 = n_heads * QK_NOPE + pp * LANES
        o2 = o1 + n_pairs * LANES
        qr = _rope_rows(z[:, o1:o1 + LANES], z[:, o2:o2 + LANES], rg_ref[...], rgs_ref[...], cos, sin,
                        same_halves=False)
        q_ref[pp] = jnp.concatenate(qn + [qr], axis=-1).astype(BF16)


def _q_proj(x, g, w_dq, l_g, w_q, n_g, r_g, r_gs, cos, sin, n_heads, scale):
    b, s, d = x.shape
    tm = _row_tile(s, STREAM_ROWS)
    q_lora = w_dq.shape[1]
    n_pairs = n_heads // 2
    dq = 2 * QK_NOPE + LANES
    tab = pl.BlockSpec((tm, LANES), lambda bi, j: (j, 0))
    return pl.pallas_call(
        functools.partial(_q_kernel, n_heads=n_heads),
        out_shape=jax.ShapeDtypeStruct((b, n_pairs, s, dq), BF16),
        grid=(b, s // tm),
        in_specs=[pl.BlockSpec((None, tm, d), lambda bi, j: (bi, j, 0)), _resident((1, d)),
                  _resident(w_dq), _resident((1, q_lora)), _resident(w_q),
                  _resident((1, QK_NOPE)), _resident((1, LANES)), _resident((1, LANES)), tab, tab],
        out_specs=pl.BlockSpec((None, n_pairs, tm, dq), lambda bi, j: (bi, 0, j, 0)),
        compiler_params=_params("parallel", "parallel"),
        name="mla_q",
    )(x, g, _arr(w_dq), l_g, _arr(w_q), n_g * scale, r_g * scale, r_gs * scale, cos, sin)


def _attn_kernel(q_ref, k_ref, v_ref, o_ref, vx_ref, *, hg, tq, ts, tk, nq, start, past):
    @pl.when(pl.program_id(2) == 0)
    def _():
        vx_ref[:, :, :V_HEAD] = v_ref[...]
        vx_ref[:, :, V_HEAD:] = jnp.ones(v_ref.shape, BF16)

    def attend(ti):
        chains = []
        for sub in range(tq // ts):
            si = ti * (tq // ts) + sub
            nk = (_full_blocks(si, ts, tk, start, past) + 1) * tk
            qpos = start + si * ts + lax.broadcasted_iota(jnp.int32, (ts, 1), 0)
            kpos = start - past + (nk - tk) + lax.broadcasted_iota(jnp.int32, (1, tk), 1)
            visible = jnp.right_shift(kpos, CHUNK_SHIFT) <= jnp.right_shift(qpos, CHUNK_SHIFT)
            chains += [(sub * ts, hh, nk, visible) for hh in range(hg)]
        scores, probs = [], []
        for r0, hh, nk, _ in chains:
            lo = (hh % 2) * QK_NOPE
            q = jnp.concatenate([q_ref[hh // 2, r0:r0 + ts, lo:lo + QK_NOPE],
                                 q_ref[hh // 2, r0:r0 + ts, 2 * QK_NOPE:]], axis=-1)
            scores.append(lax.dot_general(q, k_ref[hh, :nk, :], (((1,), (1,)), ((), ())),
                                          preferred_element_type=F32))
        for (_, _, nk, visible), s in zip(chains, scores):
            last = jnp.where(visible, s[:, nk - tk:], -jnp.inf)
            m = jnp.max(last, axis=-1, keepdims=True)
            if nk > tk:
                m = jnp.maximum(m, jnp.max(s[:, :nk - tk], axis=-1, keepdims=True))
                p = jnp.concatenate([jnp.exp2(s[:, :nk - tk] - m), jnp.exp2(last - m)], axis=-1)
            else:
                p = jnp.exp2(last - m)
            probs.append(p.astype(BF16))
        for (r0, hh, nk, _), p in zip(chains, probs):
            pv = jnp.dot(p, vx_ref[hh, :nk, :], preferred_element_type=F32)
            o_ref[r0:r0 + ts, hh * V_HEAD:(hh + 1) * V_HEAD] = (pv[:, :V_HEAD] / pv[:, V_HEAD:]).astype(BF16)

    if nq == 1:
        attend(0)
    else:
        i = pl.program_id(2)
        for ti in range(nq):
            pl.when(i == ti)(functools.partial(attend, ti))


def _full_blocks(i, tq, tk, start, past):
    first_chunk_start = ((start + i * tq) >> CHUNK_SHIFT) << CHUNK_SHIFT
    return (first_chunk_start - (start - past)) // tk


def _check_block_plan(nq, tq, tk, n_keys, start, past):
    for i in range(nq):
        n = _full_blocks(i, tq, tk, start, past)
        assert 0 <= n and (n + 1) * tk <= n_keys, (i, n)
        qc = (start + i * tq + np.arange(tq)) // CHUNK
        kc = (start - past + np.arange(n_keys)) // CHUNK
        vis = kc[None, :] <= qc[:, None]
        assert vis[:, :n * tk].all() and not vis[:, (n + 1) * tk:].any(), (i, n)
        assert vis[:, n * tk:(n + 1) * tk].any(axis=1).all()


def _attention(q, k, v, start, past):
    b, n_pairs, s, dq = q.shape
    n_heads, n_keys, dk = k.shape[1:]
    assert n_heads == 2 * n_pairs
    tq = _row_tile(s, ATTN_ROWS)
    ts = min(tq, ATTN_SUB_ROWS)
    tk = ts if n_keys % ts == 0 and s > ts else n_keys
    nq = s // tq
    hg = 4 if n_heads % 4 == 0 else 2
    _check_block_plan(s // ts, ts, tk, n_keys, start, past)
    return pl.pallas_call(
        functools.partial(_attn_kernel, hg=hg, tq=tq, ts=ts, tk=tk, nq=nq, start=start, past=past),
        out_shape=jax.ShapeDtypeStruct((b, s, n_heads * V_HEAD), BF16),
        grid=(b, n_heads // hg, nq),
        in_specs=[pl.BlockSpec((None, hg // 2, tq, dq), lambda bi, gi, i: (bi, gi, i, 0)),
                  pl.BlockSpec((None, hg, n_keys, dk), lambda bi, gi, i: (bi, gi, 0, 0)),
                  pl.BlockSpec((None, hg, n_keys, V_HEAD), lambda bi, gi, i: (bi, gi, 0, 0))],
        out_specs=pl.BlockSpec((None, tq, hg * V_HEAD), lambda bi, gi, i: (bi, i, gi)),
        scratch_shapes=[pltpu.VMEM((hg, n_keys, 2 * V_HEAD), BF16)],
        compiler_params=_params("parallel", "parallel", "arbitrary"),
        name="mla_attn",
    )(q, k, v)


def _twice(a):
    return jnp.concatenate([a, a], axis=-1)


_ROPE_SWAP = np.concatenate([np.arange(QK_ROPE // 2, QK_ROPE), np.arange(QK_ROPE // 2)])


def _rope_tables(start, s):
    inv = ROPE_BASE ** (-(jnp.arange(0, QK_ROPE, 2, dtype=F32) / QK_ROPE))
    ang = (start + jnp.arange(s, dtype=jnp.int32)).astype(F32)[:, None] * inv[None, :]
    cos, sin = jnp.cos(ang), jnp.sin(ang)
    return _twice(jnp.concatenate([cos, cos], axis=-1)), _twice(jnp.concatenate([-sin, sin], axis=-1))


def _prep_weights(p):
    n_heads = p["w_uk"].shape[1] // QK_NOPE
    assert n_heads % 2 == 0
    w = {"n_heads": n_heads}
    row = lambda a: a.reshape(1, -1).astype(F32)
    for name in ("ffn1_w_in", "ffn1_w_out", "ffn2_w_in", "ffn2_w_out", "pool_w", "w_dq", "w_o"):
        w[name] = p[name].astype(BF16)
    w["w_lat"] = jnp.concatenate(
        [p["w_dkv"], _twice(p["w_kr"]), _twice(p["w_kr"][:, _ROPE_SWAP])], axis=1).astype(BF16)
    w["kr_g"] = row(_twice(p["kr_norm"]))
    w["kr_gs"] = row(_twice(p["kr_norm"][_ROPE_SWAP]))
    w["w_kv"] = jnp.concatenate([p["w_uk"], p["w_uv"]], axis=1).astype(BF16)
    n_b, q_lora, _ = p["w_uq"].shape
    wq = p["w_uq"].reshape(n_b, q_lora, n_heads, QK_NOPE + QK_ROPE)
    nope, rope = wq[..., :QK_NOPE], wq[..., QK_NOPE:]
    flat = lambda a: a.reshape(n_b, q_lora, -1)
    w["w_q"] = jnp.concatenate([flat(nope), flat(rope), flat(rope[..., _ROPE_SWAP])], axis=-1).astype(BF16)
    w["qr_g"] = _twice(p["qr_norm"])
    w["qr_gs"] = _twice(p["qr_norm"][:, _ROPE_SWAP])
    return w


def _trunk(x, pool_prev, ckv_past, kr_past, start, p, w):
    b, s, d = x.shape
    past = 0 if ckv_past is None else ckv_past.shape[1]
    depth = p["ffn1_norm"].shape[0]
    n_a = p["pool_w"].shape[0]
    n_heads = w["n_heads"]
    row = lambda a: a.reshape(1, -1).astype(F32)
    cos, sin = _rope_tables(start, s)
    scale = float((QK_NOPE + QK_ROPE) ** -0.5 * np.log2(np.e))
    xf = x.reshape(b * s, d)
    new_pool = []
    for layer in range(depth):
        if layer == n_a:
            c_new, kr_new = _latent(xf, s, row(p["kv_norm"]), w["w_lat"], row(p["c_norm"]), w["kr_g"], w["kr_gs"],
                                    cos, sin)
            c_new = c_new.reshape(b, s, -1)
            kr_new = kr_new.reshape(b, s, LANES)
            c_all, kr_all = c_new, kr_new
            if past:
                c_all = jnp.concatenate([ckv_past.astype(F32), c_new], axis=1)
                kr_all = jnp.concatenate([_twice(kr_past.astype(F32)), kr_new], axis=1)
            keys, vals = _kv_expand(c_all, kr_all, w["w_kv"], row(p["kn_norm"]), n_heads)
        at = lambda name, idx=layer: _Stacked(w[name], idx)
        xf = _ffn(xf, row(p["ffn1_norm"][layer]), at("ffn1_w_in"), at("ffn1_w_out"))
        ffn2 = (row(p["ffn2_norm"][layer]), at("ffn2_w_in"), at("ffn2_w_out"))
        if layer < n_a:
            prev = jnp.zeros((b, POOL_HIST, d), F32) if pool_prev is None else \
                jnp.pad(pool_prev[layer].astype(F32), ((0, 0), (POOL_HIST - pool_prev.shape[2], 0), (0, 0)))
            fuse = s % FFN_ROWS == 0
            xm, st = _pool_mix(xf.reshape(b, s, d), prev, row(p["mix_norm"][layer]), at("pool_w"),
                               row(p["pool_scale"][layer]), start, ffn=ffn2 if fuse else None)
            new_pool.append(st[:, 1:])
            xf = xm.reshape(b * s, d) if fuse else _ffn(xm.reshape(b * s, d), *ffn2)
        else:
            i = layer - n_a
            q = _q_proj(xf.reshape(b, s, d), row(p["mix_norm"][layer]), at("w_dq", i), row(p["q_lat_norm"][i]),
                        at("w_q", i), row(p["qn_norm"][i]), row(w["qr_g"][i]), row(w["qr_gs"][i]), cos, sin,
                        n_heads, scale)
            att = _attention(q, keys, vals, start, past)
            xf = _ffn(xf, *ffn2, attn=att.reshape(b * s, -1), w_o=at("w_o", i))
    return xf.reshape(b, s, d), jnp.stack(new_pool, axis=0), c_new, kr_new[..., :QK_ROPE]


def kernel(x_prompt, x_sample, state_pool, cache_ckv, cache_krope, ffn1_norm, ffn1_w_in, ffn1_w_out, mix_norm, ffn2_norm, ffn2_w_in, ffn2_w_out, pool_w, pool_scale, kv_norm, w_dkv, c_norm, w_kr, kr_norm, w_uk, kn_norm, w_uv, w_dq, q_lat_norm, w_uq, qn_norm, qr_norm, w_o):
    p = dict(ffn1_norm=ffn1_norm, ffn1_w_in=ffn1_w_in, ffn1_w_out=ffn1_w_out, mix_norm=mix_norm,
             ffn2_norm=ffn2_norm, ffn2_w_in=ffn2_w_in, ffn2_w_out=ffn2_w_out, pool_w=pool_w,
             pool_scale=pool_scale, kv_norm=kv_norm, w_dkv=w_dkv, c_norm=c_norm, w_kr=w_kr, kr_norm=kr_norm,
             w_uk=w_uk, kn_norm=kn_norm, w_uv=w_uv, w_dq=w_dq, q_lat_norm=q_lat_norm, w_uq=w_uq,
             qn_norm=qn_norm, qr_norm=qr_norm, w_o=w_o)
    w = _prep_weights(p)
    y_p, pool_p, ckv_p, kr_p = _trunk(x_prompt, None, None, None, 0, p, w)
    y_s, pool_s, ckv_s, kr_s = _trunk(x_sample, state_pool, cache_ckv, cache_krope, cache_ckv.shape[1], p, w)
    return (y_p, y_s, pool_p, pool_s, ckv_p, kr_p, ckv_s, kr_s)
```
